```python
import jax, jax.numpy as jnp
from jax import lax
import numpy as np

D_MODEL = 2048
BATCH = 2
SEQ = 8192
DEPTH = 1

HEAD_DIM = 128
N_SB_HEADS = 8
N_DSA_HEADS = 8
N_IDX_HEADS = 16
IDX_DIM = 64
ROPE_THETA = 500000.0
ROPE_DIM = HEAD_DIM // 4
IDX_ROPE_DIM = IDX_DIM // 4
MAX_TOPK = 256
D_FF = 5632
CONV_WIDTH = 3
Q_BLOCK = 128
EPS = 1e-6

SB_W = N_SB_HEADS * HEAD_DIM
DSA_W = N_DSA_HEADS * HEAD_DIM
IDX_QW = N_IDX_HEADS * IDX_DIM
IN_SPLITS = (SB_W, SB_W, SB_W, DSA_W, DSA_W, DSA_W, IDX_QW, IDX_DIM, N_IDX_HEADS, D_MODEL, D_MODEL)
IN_COLS = sum(IN_SPLITS)
IN_OFFSETS = tuple(int(v) for v in np.cumsum(IN_SPLITS)[:-1])

kernel_name = "hybrid_stickbreak_dsa_convffn"


def rms_norm(x, g):
    xf = x.astype(jnp.float32)
    y = xf * lax.rsqrt(jnp.mean(xf * xf, axis=-1, keepdims=True) + EPS)
    return (y * g.astype(jnp.float32)).astype(x.dtype)


def partial_rope(x, positions, rot_dim):
    half = rot_dim // 2
    freqs = ROPE_THETA ** (-jnp.arange(0, rot_dim, 2, dtype=jnp.float32) / rot_dim)
    ang = positions.astype(jnp.float32)[..., None] * freqs
    cos = jnp.cos(ang)[:, :, None, :].astype(x.dtype)
    sin = jnp.sin(ang)[:, :, None, :].astype(x.dtype)
    x1, x2, rest = x[..., :half], x[..., half:rot_dim], x[..., rot_dim:]
    return jnp.concatenate([x1 * cos - x2 * sin, x1 * sin + x2 * cos, rest], axis=-1)


def to_blocks(a):
    b, s = a.shape[:2]
    return a.reshape(b, s // Q_BLOCK, Q_BLOCK, *a.shape[2:]).swapaxes(0, 1)


def from_blocks(a):
    nb, b, qb = a.shape[:3]
    return a.swapaxes(0, 1).reshape(b, nb * qb, *a.shape[3:])


def stick_breaking_attention(q, k, v):
    S = q.shape[1]
    scale = HEAD_DIM ** -0.5
    key_pos = jnp.arange(S)
    starts = jnp.arange(S // Q_BLOCK, dtype=jnp.int32) * Q_BLOCK

    def block(args):
        qb, start = args
        z = jnp.einsum('bqhd,bshd->bhqs', qb, k).astype(jnp.float32) * scale
        t = start + jnp.arange(Q_BLOCK)
        mask = key_pos[None, :] < t[:, None]
        log_beta = jax.nn.log_sigmoid(z)
        log_1mb = jnp.where(mask, jax.nn.log_sigmoid(-z), 0.0)
        suffix = lax.cumsum(log_1mb, axis=3, reverse=True) - log_1mb
        a = jnp.where(mask, jnp.exp(log_beta + suffix), 0.0)
        return jnp.einsum('bhqs,bshd->bqhd', a.astype(v.dtype), v)

    out = lax.map(block, (to_blocks(q), starts))
    return from_blocks(out)


def indexed_sparse_attention(q, k, v, iq, ik, iw, topk):
    B, S = q.shape[:2]
    scale = HEAD_DIM ** -0.5
    idx_scale = IDX_DIM ** -0.5 * N_IDX_HEADS ** -0.5
    key_pos = jnp.arange(S)
    bidx = jnp.arange(B)[:, None, None]
    starts = jnp.arange(S // Q_BLOCK, dtype=jnp.int32) * Q_BLOCK

    def block(args):
        qb, iqb, iwb, start = args
        t = start + jnp.arange(Q_BLOCK)
        sc = jnp.einsum('bqhd,bsd->bqhs', iqb, ik).astype(jnp.float32)
        score = jnp.einsum('bqhs,bqh->bqs', jax.nn.relu(sc), iwb.astype(jnp.float32) * idx_scale)
        causal = key_pos[None, :] <= t[:, None]
        score = jnp.where(causal[None], score, -jnp.inf)
        _, sel = lax.top_k(score, topk)
        valid = sel <= t[None, :, None]
        k_sel = k[bidx, sel]
        v_sel = v[bidx, sel]
        logits = jnp.einsum('bqhd,bqkhd->bhqk', qb, k_sel).astype(jnp.float32) * scale
        logits = jnp.where(valid[:, None], logits, -jnp.inf)
        p = jax.nn.softmax(logits, axis=-1)
        return jnp.einsum('bhqk,bqkhd->bqhd', p.astype(v.dtype), v_sel)

    out = lax.map(block, (to_blocks(q), to_blocks(iq), to_blocks(iw), starts))
    return from_blocks(out)


def causal_depthwise_conv(u, w, b):
    C = u.shape[-1]
    y = lax.conv_general_dilated(u, w[:, None, :].astype(u.dtype), window_strides=(1,),
                                 padding=[(CONV_WIDTH - 1, 0)],
                                 dimension_numbers=('NWC', 'WIO', 'NWC'),
                                 feature_group_count=C)
    return y + b


def setup_inputs(seed: int = 0) -> dict:
    key = jax.random.key(seed)
    ks = jax.random.split(key, 13)
    f32 = jnp.float32
    x = jax.random.normal(ks[0], (BATCH, SEQ, D_MODEL), f32)
    positions = jnp.broadcast_to(jnp.arange(SEQ, dtype=jnp.int32), (BATCH, SEQ))
    attn_norm_g = 1.0 + 0.02 * jax.random.normal(ks[1], (D_MODEL,), f32)
    w_in = jax.random.normal(ks[2], (D_MODEL, IN_COLS), f32) * D_MODEL ** -0.5
    q_norm_g = 1.0 + 0.02 * jax.random.normal(ks[3], (HEAD_DIM,), f32)
    k_norm_g = 1.0 + 0.02 * jax.random.normal(ks[4], (HEAD_DIM,), f32)
    w_branch = jax.random.normal(ks[5], (SB_W + DSA_W, D_MODEL), f32) * SB_W ** -0.5
    w_out = jax.random.normal(ks[6], (D_MODEL, D_MODEL), f32) * D_MODEL ** -0.5
    ffn_norm_g = 1.0 + 0.02 * jax.random.normal(ks[7], (D_MODEL,), f32)
    w_up = jax.random.normal(ks[8], (D_MODEL, 2 * D_FF), f32) * D_MODEL ** -0.5
    conv_w = jax.random.normal(ks[9], (CONV_WIDTH, 2 * D_FF), f32) * CONV_WIDTH ** -0.5
    conv_b = 0.01 * jax.random.normal(ks[10], (2 * D_FF,), f32)
    w_down = jax.random.normal(ks[11], (D_FF, D_MODEL), f32) * D_FF ** -0.5
    return {"x": x, "positions": positions, "attn_norm_g": attn_norm_g, "w_in": w_in,
            "q_norm_g": q_norm_g, "k_norm_g": k_norm_g, "w_branch": w_branch, "w_out": w_out,
            "ffn_norm_g": ffn_norm_g, "w_up": w_up, "conv_w": conv_w, "conv_b": conv_b,
            "w_down": w_down}


def reference(x, positions, attn_norm_g, w_in, q_norm_g, k_norm_g, w_branch, w_out,
              ffn_norm_g, w_up, conv_w, conv_b, w_down):
    B, S, _ = x.shape
    topk = min(MAX_TOPK, S // 4)
    for _layer in range(DEPTH):
        h = rms_norm(x, attn_norm_g)
        proj = h @ w_in
        (q_sb, k_sb, v_sb, q_ds, k_ds, v_ds, q_ix, k_ix, w_ix, g_sb, g_ds) = jnp.split(proj, IN_OFFSETS, axis=-1)
        heads = lambda a, n, d: a.reshape(B, S, n, d)

        o_sb = stick_breaking_attention(heads(q_sb, N_SB_HEADS, HEAD_DIM),
                                        heads(k_sb, N_SB_HEADS, HEAD_DIM),
                                        heads(v_sb, N_SB_HEADS, HEAD_DIM)).reshape(B, S, SB_W)

        qd = partial_rope(rms_norm(heads(q_ds, N_DSA_HEADS, HEAD_DIM), q_norm_g), positions, ROPE_DIM)
        kd = partial_rope(rms_norm(heads(k_ds, N_DSA_HEADS, HEAD_DIM), k_norm_g), positions, ROPE_DIM)
        vd = heads(v_ds, N_DSA_HEADS, HEAD_DIM)
        iq = partial_rope(heads(q_ix, N_IDX_HEADS, IDX_DIM), positions, IDX_ROPE_DIM)
        ik = partial_rope(k_ix[:, :, None, :], positions, IDX_ROPE_DIM)[:, :, 0, :]
        o_ds = indexed_sparse_attention(qd, kd, vd, iq, ik, w_ix, topk).reshape(B, S, DSA_W)

        merged = (jax.nn.sigmoid(g_sb) * (o_sb @ w_branch[:SB_W])
                  + jax.nn.sigmoid(g_ds) * (o_ds @ w_branch[SB_W:]))
        x = x + merged @ w_out

        h = rms_norm(x, ffn_norm_g)
        u = causal_depthwise_conv(h @ w_up, conv_w, conv_b)
        gate, val = jnp.split(u, 2, axis=-1)
        x = x + (jax.nn.silu(gate) * val) @ w_down
    return x
```

```python
import functools

import jax
import jax.numpy as jnp
from jax import lax
from jax.experimental import pallas as pl
from jax.experimental.pallas import tpu as pltpu

D_MODEL = 2048
HEAD_DIM = 128
N_SB_HEADS = 8
N_DSA_HEADS = 8
N_IDX_HEADS = 16
IDX_DIM = 64
ROPE_THETA = 500000.0
ROPE_DIM = HEAD_DIM // 4
IDX_ROPE_DIM = IDX_DIM // 4
MAX_TOPK = 256
D_FF = 5632
CONV_WIDTH = 3
EPS = 1e-6

SB_W = N_SB_HEADS * HEAD_DIM
DSA_W = N_DSA_HEADS * HEAD_DIM
IDX_QW = N_IDX_HEADS * IDX_DIM

LANES = 128
VMEM_LIMIT = 56 * 1024 * 1024
INT_MIN = -(2 ** 31)
NEG_BIG = -1e30
EXP_UNDERFLOW = -105.0

F32 = jnp.float32
BF16 = jnp.bfloat16


def _params(semantics):
    return pltpu.CompilerParams(dimension_semantics=semantics, vmem_limit_bytes=VMEM_LIMIT)


def _dot(a, b):
    return jnp.dot(a, b, preferred_element_type=F32)


def _dot_nt(a, b):
    return lax.dot_general(a, b, (((1,), (1,)), ((), ())), preferred_element_type=F32)


def _rmsnorm_kernel(x_ref, g_ref, o_ref):
    x = x_ref[...]
    ms = jnp.mean(x * x, axis=-1, keepdims=True)
    o_ref[...] = (x * lax.rsqrt(ms + EPS) * g_ref[...]).astype(o_ref.dtype)


def _rmsnorm(x, g, tm=512):
    T, D = x.shape
    return pl.pallas_call(
        _rmsnorm_kernel,
        out_shape=jax.ShapeDtypeStruct((T, D), BF16),
        grid=(T // tm,),
        in_specs=[pl.BlockSpec((tm, D), lambda i: (i, 0)),
                  pl.BlockSpec((1, D), lambda i: (0, 0))],
        out_specs=pl.BlockSpec((tm, D), lambda i: (i, 0)),
        compiler_params=_params(("parallel",)),
        name="rmsnorm",
    )(x, g.reshape(1, D))


def _rope_head(xh, cos_t, sin_up, sin_dn, half):
    return (xh * cos_t + pltpu.roll(xh, half, 1) * sin_up
            + pltpu.roll(xh, LANES - half, 1) * sin_dn)


def _proj_attn_kernel(a_ref, w_ref, tab_ref, qg_ref, kg_ref, o_ref, *, tn):
    acc = _dot(a_ref[...], w_ref[...])
    region = pl.program_id(1) // (SB_W // tn)
    heads = tn // LANES

    def norm_rope(g_ref):
        for h in range(heads):
            xh = acc[:, h * LANES:(h + 1) * LANES]
            ms = jnp.mean(xh * xh, axis=-1, keepdims=True)
            xh = xh * lax.rsqrt(ms + EPS) * g_ref[...]
            out = _rope_head(xh, tab_ref[:, 0:LANES], tab_ref[:, LANES:2 * LANES],
                             tab_ref[:, 2 * LANES:3 * LANES], ROPE_DIM // 2)
            o_ref[:, h * LANES:(h + 1) * LANES] = out.astype(o_ref.dtype)

    @pl.when((region <= 2) | (region == 5))
    def _():
        o_ref[...] = acc.astype(o_ref.dtype)

    @pl.when(region == 3)
    def _():
        norm_rope(qg_ref)

    @pl.when(region == 4)
    def _():
        norm_rope(kg_ref)

    @pl.when(region == 6)
    def _():
        for h in range(heads):
            xh = acc[:, h * LANES:(h + 1) * LANES]
            out = _rope_head(xh, tab_ref[:, 3 * LANES:4 * LANES], tab_ref[:, 4 * LANES:5 * LANES],
                             tab_ref[:, 5 * LANES:6 * LANES], IDX_ROPE_DIM // 2)
            o_ref[:, h * LANES:(h + 1) * LANES] = out.astype(o_ref.dtype)


def _proj_attn(h, w, tab, qg, kg, tm=512, tn=512):
    T, K = h.shape
    N = w.shape[1]
    return pl.pallas_call(
        functools.partial(_proj_attn_kernel, tn=tn),
        out_shape=jax.ShapeDtypeStruct((T, N), BF16),
        grid=(T // tm, N // tn),
        in_specs=[pl.BlockSpec((tm, K), lambda i, j: (i, 0)),
                  pl.BlockSpec((K, tn), lambda i, j: (0, j)),
                  pl.BlockSpec((tm, 6 * LANES), lambda i, j: (i, 0)),
                  pl.BlockSpec((1, LANES), lambda i, j: (0, 0)),
                  pl.BlockSpec((1, LANES), lambda i, j: (0, 0))],
        out_specs=pl.BlockSpec((tm, tn), lambda i, j: (i, j)),
        compiler_params=_params(("parallel", "arbitrary")),
        name="proj_attn",
    )(h, w, tab, qg.reshape(1, LANES), kg.reshape(1, LANES))


def _matmul_kernel(a_ref, w_ref, o_ref):
    o_ref[...] = _dot(a_ref[...], w_ref[...]).astype(o_ref.dtype)


def _matmul(a, w, out_dtype, tm=512, tn=512, name="matmul"):
    T, K = a.shape
    N = w.shape[1]
    return pl.pallas_call(
        _matmul_kernel,
        out_shape=jax.ShapeDtypeStruct((T, N), out_dtype),
        grid=(T // tm, N // tn),
        in_specs=[pl.BlockSpec((tm, K), lambda i, j: (i, 0)),
                  pl.BlockSpec((K, tn), lambda i, j: (0, j))],
        out_specs=pl.BlockSpec((tm, tn), lambda i, j: (i, j)),
        compiler_params=_params(("parallel", "arbitrary")),
        name=name,
    )(a, w)


def _proj_idx_kernel(a_ref, w_ref, tab_ref, ik_ref, iw_ref):
    acc = _dot(a_ref[...], w_ref[...])
    kk = acc[:, 0:LANES]
    out = _rope_head(kk, tab_ref[:, 3 * LANES:4 * LANES], tab_ref[:, 4 * LANES:5 * LANES],
                     tab_ref[:, 5 * LANES:6 * LANES], IDX_ROPE_DIM // 2)
    ik_ref[...] = out.astype(ik_ref.dtype)
    iw_ref[...] = acc[:, LANES:2 * LANES]


def _proj_idx(h, w, tab, tm=512):
    T, K = h.shape
    return pl.pallas_call(
        _proj_idx_kernel,
        out_shape=(jax.ShapeDtypeStruct((T, LANES), BF16), jax.ShapeDtypeStruct((T, LANES), F32)),
        grid=(T // tm,),
        in_specs=[pl.BlockSpec((tm, K), lambda i: (i, 0)),
                  pl.BlockSpec((K, 2 * LANES), lambda i: (0, 0)),
                  pl.BlockSpec((tm, 6 * LANES), lambda i: (i, 0))],
        out_specs=(pl.BlockSpec((tm, LANES), lambda i: (i, 0)),
                   pl.BlockSpec((tm, LANES), lambda i: (i, 0))),
        compiler_params=_params(("parallel",)),
        name="proj_idx",
    )(h, w, tab)


def _sb_kernel(q_ref, k_ref, v_ref, o_ref, acc_ref, carry_ref, *, tq, tk, scale):
    qi = pl.program_id(2)
    q = q_ref[...]
    acc_ref[...] = jnp.zeros_like(acc_ref)
    carry_ref[...] = jnp.zeros_like(carry_ref)
    tri = (lax.broadcasted_iota(jnp.int32, (tk, tk), 0)
           > lax.broadcasted_iota(jnp.int32, (tk, tk), 1)).astype(BF16)
    t_idx = qi * tq + lax.broadcasted_iota(jnp.int32, (tq, tk), 0)
    col = lax.broadcasted_iota(jnp.int32, (tq, tk), 1)

    def body(state):
        j, _ = state
        off = pl.multiple_of(j * tk, tk)
        kb = k_ref[pl.ds(off, tk), :]
        vb = v_ref[pl.ds(off, tk), :]
        z = _dot_nt(q, kb) * scale
        mask = (off + col) < t_idx
        e = jnp.exp(-jnp.abs(z))
        log_beta = jnp.minimum(z, 0.0) - jnp.log(1.0 + e)
        l1 = jnp.where(mask, log_beta - z, 0.0)
        hi = l1.astype(BF16)
        lo = (l1 - hi.astype(F32)).astype(BF16)
        suffix = _dot(hi, tri) + _dot(lo, tri)
        carry = carry_ref[...]
        a = jnp.where(mask, jnp.exp(log_beta + suffix + carry), 0.0)
        acc_ref[...] += _dot(a.astype(BF16), vb)
        new_carry = carry + jnp.sum(l1, axis=1, keepdims=True)
        carry_ref[...] = new_carry
        alive = (jnp.max(new_carry) > EXP_UNDERFLOW).astype(jnp.int32)
        return j - 1, alive

    j_start = (qi + 1) * (tq // tk) - 1
    lax.while_loop(lambda s: (s[0] >= 0) & (s[1] > 0), body, (j_start, jnp.int32(1)))
    o_ref[...] = acc_ref[...].astype(o_ref.dtype)


def _sb_attention(pa, B, S, tq=256, tk=128):
    H = N_SB_HEADS
    return pl.pallas_call(
        functools.partial(_sb_kernel, tq=tq, tk=tk, scale=HEAD_DIM ** -0.5),
        out_shape=jax.ShapeDtypeStruct((B, S, SB_W), BF16),
        grid=(B, H, S // tq),
        in_specs=[pl.BlockSpec((None, tq, HEAD_DIM), lambda b, h, i: (b, i, h)),
                  pl.BlockSpec((None, S, HEAD_DIM), lambda b, h, i: (b, 0, H + h)),
                  pl.BlockSpec((None, S, HEAD_DIM), lambda b, h, i: (b, 0, 2 * H + h))],
        out_specs=pl.BlockSpec((None, tq, HEAD_DIM), lambda b, h, i: (b, i, h)),
        scratch_shapes=[pltpu.VMEM((tq, HEAD_DIM), F32), pltpu.VMEM((tq, 1), F32)],
        compiler_params=_params(("parallel", "parallel", "arbitrary")),
        name="sb_attention",
    )(pa, pa, pa)


def _indexer_kernel(iq_ref, iw_ref, ik_ref, mask_ref, lhs_ref, key_ref, *, tq, tk, topk, n_kc, idx_scale):
    qi = pl.program_id(1)
    n_causal = ((qi + 1) * tq + tk - 1) // tk
    low = lax.broadcasted_iota(jnp.int32, (tq, LANES), 1) < IDX_DIM
    for p in range(N_IDX_HEADS // 2):
        pair = iq_ref[:, p * LANES:(p + 1) * LANES]
        lhs_ref[2 * p] = jnp.where(low, pair, jnp.zeros_like(pair))
        lhs_ref[2 * p + 1] = jnp.where(low, jnp.zeros_like(pair), pair)
    w = iw_ref[...] * idx_scale
    t_col = qi * tq + lax.broadcasted_iota(jnp.int32, (tq, 1), 0)
    t_idx = qi * tq + lax.broadcasted_iota(jnp.int32, (tq, tk), 0)
    col = lax.broadcasted_iota(jnp.int32, (tq, tk), 1)

    def score_chunk(c, _):
        off = pl.multiple_of(c * tk, tk)
        kb = ik_ref[pl.ds(off, tk), :]
        score = jnp.zeros((tq, tk), F32)
        for h in range(N_IDX_HEADS):
            sc = _dot_nt(lhs_ref[h], kb)
            score = score + jnp.maximum(sc, 0.0) * w[:, h:h + 1]
        bits = pltpu.bitcast(score, jnp.int32)
        key = bits ^ ((bits >> 31) & jnp.int32(0x7FFFFFFF))
        key_ref[c] = jnp.where((off + col) <= t_idx, key, jnp.int32(INT_MIN))
        return 0

    lax.fori_loop(0, n_causal, score_chunk, 0)

    keff = jnp.minimum(t_col + 1, topk)

    def count_ge(cand):
        def body(c, acc):
            k = key_ref[c]
            for l in range(tk // LANES):
                acc = acc + (k[:, l * LANES:(l + 1) * LANES] >= cand).astype(jnp.int32)
            return acc
        acc = lax.fori_loop(0, n_causal, body, jnp.zeros((tq, LANES), jnp.int32))
        return jnp.sum(acc.astype(F32), axis=1, keepdims=True).astype(jnp.int32)

    zero = jnp.zeros((tq, 1), jnp.int32)
    ans = jnp.where(count_ge(zero) >= keff, zero, jnp.int32(INT_MIN))

    def bit_step(i, ans):
        cand = ans | (jnp.int32(1) << (30 - i))
        return jnp.where(count_ge(cand) >= keff, cand, ans)

    thr = lax.fori_loop(0, 31, bit_step, ans)

    def write_chunk(c, _):
        off = pl.multiple_of(c * tk, tk)
        sel = (key_ref[c] >= thr) & ((off + col) <= t_idx)
        mask_ref[c] = jnp.where(sel, 1, 0).astype(mask_ref.dtype)
        return 0

    lax.fori_loop(0, n_causal, write_chunk, 0)

    def zero_chunk(c, _):
        mask_ref[c] = jnp.zeros((tq, tk), mask_ref.dtype)
        return 0

    lax.fori_loop(n_causal, n_kc, zero_chunk, 0)


def _indexer(pa, ik2, iw, B, S, topk, tq=256, tk=512):
    n_kc = S // tk
    return pl.pallas_call(
        functools.partial(_indexer_kernel, tq=tq, tk=tk, topk=topk, n_kc=n_kc,
                          idx_scale=IDX_DIM ** -0.5 * N_IDX_HEADS ** -0.5),
        out_shape=jax.ShapeDtypeStruct((B, n_kc, S, tk), jnp.int8),
        grid=(B, S // tq),
        in_specs=[pl.BlockSpec((None, tq, IDX_QW), lambda b, i: (b, i, 6)),
                  pl.BlockSpec((None, tq, LANES), lambda b, i: (b, i, 0)),
                  pl.BlockSpec((None, S, LANES), lambda b, i: (b, 0, 0))],
        out_specs=pl.BlockSpec((None, n_kc, tq, tk), lambda b, i: (b, 0, i, 0)),
        scratch_shapes=[pltpu.VMEM((N_IDX_HEADS, tq, LANES), BF16),
                        pltpu.VMEM((n_kc, tq, tk), jnp.int32)],
        compiler_params=_params(("parallel", "arbitrary")),
        name="indexer",
    )(pa, iw, ik2)


def _dsa_kernel(q_ref, k_ref, v_ref, m_ref, o_ref, acc_ref, mx_ref, l_ref, *, t, scale):
    qi = pl.program_id(1)
    kj = pl.program_id(2)

    @pl.when(kj == 0)
    def _():
        acc_ref[...] = jnp.zeros_like(acc_ref)
        mx_ref[...] = jnp.full_like(mx_ref, NEG_BIG)
        l_ref[...] = jnp.zeros_like(l_ref)

    @pl.when(kj <= qi)
    def _():
        sel = m_ref[...] != 0
        for h in range(N_DSA_HEADS):
            hs = slice(h * HEAD_DIM, (h + 1) * HEAD_DIM)
            s = _dot_nt(q_ref[:, hs], k_ref[:, hs]) * scale
            s = jnp.where(sel, s, NEG_BIG)
            m_old = mx_ref[h]
            m_new = jnp.maximum(m_old, jnp.max(s, axis=1, keepdims=True))
            alpha = jnp.exp(m_old - m_new)
            p = jnp.where(sel, jnp.exp(s - m_new), 0.0)
            l_ref[h] = alpha * l_ref[h] + jnp.sum(p, axis=1, keepdims=True)
            acc_ref[:, hs] = alpha * acc_ref[:, hs] + _dot(p.astype(BF16), v_ref[:, hs])
            mx_ref[h] = m_new

    @pl.when(kj == pl.num_programs(2) - 1)
    def _():
        for h in range(N_DSA_HEADS):
            hs = slice(h * HEAD_DIM, (h + 1) * HEAD_DIM)
            o_ref[:, hs] = (acc_ref[:, hs] / l_ref[h]).astype(o_ref.dtype)


def _dsa_attention(pa, mask, B, S, t=512):
    n = S // t
    return pl.pallas_call(
        functools.partial(_dsa_kernel, t=t, scale=HEAD_DIM ** -0.5),
        out_shape=jax.ShapeDtypeStruct((B, S, DSA_W), BF16),
        grid=(B, n, n),
        in_specs=[pl.BlockSpec((None, t, DSA_W), lambda b, i, j: (b, i, 3)),
                  pl.BlockSpec((None, t, DSA_W), lambda b, i, j: (b, jnp.minimum(j, i), 4)),
                  pl.BlockSpec((None, t, DSA_W), lambda b, i, j: (b, jnp.minimum(j, i), 5)),
                  pl.BlockSpec((None, None, t, t), lambda b, i, j: (b, jnp.minimum(j, i), i, 0))],
        out_specs=pl.BlockSpec((None, t, DSA_W), lambda b, i, j: (b, i, 0)),
        scratch_shapes=[pltpu.VMEM((t, DSA_W), F32),
                        pltpu.VMEM((N_DSA_HEADS, t, 1), F32),
                        pltpu.VMEM((N_DSA_HEADS, t, 1), F32)],
        compiler_params=_params(("parallel", "parallel", "arbitrary")),
        name="dsa_attention",
    )(pa, pa, pa, mask)


def _merge_kernel(osb_ref, ods_ref, wsb_ref, wds_ref, gsb_ref, gds_ref, o_ref):
    a = _dot(osb_ref[...], wsb_ref[...])
    b = _dot(ods_ref[...], wds_ref[...])
    o_ref[...] = (jax.nn.sigmoid(gsb_ref[...]) * a + jax.nn.sigmoid(gds_ref[...]) * b).astype(o_ref.dtype)


def _merge(o_sb, o_ds, w_branch, gates, tm=512, tn=512):
    T = o_sb.shape[0]
    nj = D_MODEL // tn
    return pl.pallas_call(
        _merge_kernel,
        out_shape=jax.ShapeDtypeStruct((T, D_MODEL), BF16),
        grid=(T // tm, nj),
        in_specs=[pl.BlockSpec((tm, SB_W), lambda i, j: (i, 0)),
                  pl.BlockSpec((tm, DSA_W), lambda i, j: (i, 0)),
                  pl.BlockSpec((SB_W, tn), lambda i, j: (0, j)),
                  pl.BlockSpec((DSA_W, tn), lambda i, j: (SB_W // DSA_W, j)),
                  pl.BlockSpec((tm, tn), lambda i, j: (i, j)),
                  pl.BlockSpec((tm, tn), lambda i, j: (i, nj + j))],
        out_specs=pl.BlockSpec((tm, tn), lambda i, j: (i, j)),
        compiler_params=_params(("parallel", "arbitrary")),
        name="merge",
    )(o_sb, o_ds, w_branch, w_branch, gates, gates)


def _matmul_res_kernel(a_ref, w_ref, r_ref, o_ref):
    o_ref[...] = r_ref[...] + _dot(a_ref[...], w_ref[...])


def _matmul_res(a, w, res, tm=512, tn=512, name="matmul_res"):
    T, K = a.shape
    N = w.shape[1]
    return pl.pallas_call(
        _matmul_res_kernel,
        out_shape=jax.ShapeDtypeStruct((T, N), F32),
        grid=(T // tm, N // tn),
        in_specs=[pl.BlockSpec((tm, K), lambda i, j: (i, 0)),
                  pl.BlockSpec((K, tn), lambda i, j: (0, j)),
                  pl.BlockSpec((tm, tn), lambda i, j: (i, j))],
        out_specs=pl.BlockSpec((tm, tn), lambda i, j: (i, j)),
        compiler_params=_params(("parallel", "arbitrary")),
        name=name,
    )(a, w, res)


def _conv_gate_kernel(ug_ref, uv_ref, pg_ref, pv_ref, wg_ref, wv_ref, bg_ref, bv_ref, o_ref, *, tm, rows_per_seq):
    first = (pl.program_id(0) % (rows_per_seq // tm)) == 0
    row = lax.broadcasted_iota(jnp.int32, ug_ref.shape, 0)

    def conv(u_ref, p_ref, w_ref, b_ref):
        u = u_ref[...]
        prev = jnp.where(first, 0.0, p_ref[...])
        p1 = prev[7:8, :]
        p2 = prev[6:7, :]
        u1 = jnp.where(row == 0, p1, pltpu.roll(u, 1, 0))
        u2 = jnp.where(row == 0, p2, jnp.where(row == 1, p1, pltpu.roll(u, 2, 0)))
        return w_ref[0:1, :] * u2 + w_ref[1:2, :] * u1 + w_ref[2:3, :] * u + b_ref[...]

    g = conv(ug_ref, pg_ref, wg_ref, bg_ref)
    v = conv(uv_ref, pv_ref, wv_ref, bv_ref)
    o_ref[...] = (g * jax.nn.sigmoid(g) * v).astype(o_ref.dtype)


def _conv_gate(u, conv_w, conv_b, rows_per_seq, tm=512, tc=512):
    T = u.shape[0]
    nj = D_FF // tc
    sub = 8
    prev_map_g = lambda i, j: (jnp.maximum(i * (tm // sub) - 1, 0), j)
    prev_map_v = lambda i, j: (jnp.maximum(i * (tm // sub) - 1, 0), nj + j)
    cb = conv_b.reshape(1, 2 * D_FF)
    return pl.pallas_call(
        functools.partial(_conv_gate_kernel, tm=tm, rows_per_seq=rows_per_seq),
        out_shape=jax.ShapeDtypeStruct((T, D_FF), BF16),
        grid=(T // tm, nj),
        in_specs=[pl.BlockSpec((tm, tc), lambda i, j: (i, j)),
                  pl.BlockSpec((tm, tc), lambda i, j: (i, nj + j)),
                  pl.BlockSpec((sub, tc), prev_map_g),
                  pl.BlockSpec((sub, tc), prev_map_v),
                  pl.BlockSpec((CONV_WIDTH, tc), lambda i, j: (0, j)),
                  pl.BlockSpec((CONV_WIDTH, tc), lambda i, j: (0, nj + j)),
                  pl.BlockSpec((1, tc), lambda i, j: (0, j)),
                  pl.BlockSpec((1, tc), lambda i, j: (0, nj + j))],
        out_specs=pl.BlockSpec((tm, tc), lambda i, j: (i, j)),
        compiler_params=_params(("parallel", "parallel")),
        name="conv_gate",
    )(u, u, u, u, conv_w, conv_w, cb, cb)


def _rope_tables(positions):
    pos = positions.reshape(-1).astype(F32)[:, None]

    def tables(rot, width):
        half = rot // 2
        freqs = ROPE_THETA ** (-jnp.arange(0, rot, 2, dtype=F32) / rot)
        ang = pos * freqs
        cos, sin = jnp.cos(ang), jnp.sin(ang)
        T = pos.shape[0]
        ones = jnp.ones((T, width - rot), F32)
        zeros = jnp.zeros((T, width - rot), F32)
        zh = jnp.zeros((T, half), F32)
        c = jnp.concatenate([cos, cos, ones], axis=1)
        up = jnp.concatenate([zh, sin, zeros], axis=1)
        dn = jnp.concatenate([-sin, zh, zeros], axis=1)
        rep = LANES // width
        return [jnp.tile(a, (1, rep)) for a in (c, up, dn)]

    return jnp.concatenate(tables(ROPE_DIM, HEAD_DIM) + tables(IDX_ROPE_DIM, IDX_DIM), axis=1)


def kernel(x, positions, attn_norm_g, w_in, q_norm_g, k_norm_g, w_branch, w_out, ffn_norm_g, w_up,
           conv_w, conv_b, w_down):
    B, S, D = x.shape
    assert D == D_MODEL and S % 512 == 0
    T = B * S
    topk = min(MAX_TOPK, S // 4)
    xf = x.reshape(T, D)

    o_qix = 3 * SB_W + 3 * DSA_W
    o_kix = o_qix + IDX_QW
    o_wix = o_kix + IDX_DIM
    o_g = o_wix + N_IDX_HEADS
    w_attn = w_in[:, :o_kix].astype(BF16)
    w_kix = w_in[:, o_kix:o_wix]
    w_idx = jnp.concatenate([w_kix, w_kix, w_in[:, o_wix:o_g],
                             jnp.zeros((D, LANES - N_IDX_HEADS), w_in.dtype)], axis=1).astype(BF16)
    w_gate = w_in[:, o_g:].astype(BF16)

    tab = _rope_tables(positions)

    h1 = _rmsnorm(xf, attn_norm_g)
    pa = _proj_attn(h1, w_attn, tab, q_norm_g, k_norm_g)
    gates = _matmul(h1, w_gate, F32, name="proj_gates")
    ik2, iw = _proj_idx(h1, w_idx, tab)

    pa3 = pa.reshape(B, S, pa.shape[1])
    o_sb = _sb_attention(pa3, B, S)
    mask = _indexer(pa3, ik2.reshape(B, S, LANES), iw.reshape(B, S, LANES), B, S, topk)
    o_ds = _dsa_attention(pa3, mask, B, S)

    merged = _merge(o_sb.reshape(T, SB_W), o_ds.reshape(T, DSA_W), w_branch.astype(BF16), gates)
    x1 = _matmul_res(merged, w_out.astype(BF16), xf, name="out_proj")

    h2 = _rmsnorm(x1, ffn_norm_g)
    u = _matmul(h2, w_up.astype(BF16), F32, name="up_proj")
    act = _conv_gate(u, conv_w, conv_b, S)
    out = _matmul_res(act, w_down.astype(BF16), x1, name="down_proj")
    return out.reshape(B, S, D)
```

```python
import functools

import jax
import jax.numpy as jnp
from jax import lax
from jax.experimental import pallas as pl
from jax.experimental.pallas import tpu as pltpu

D_MODEL = 2048
HEAD_DIM = 128
N_SB_HEADS = 8
N_DSA_HEADS = 8
N_IDX_HEADS = 16
IDX_DIM = 64
ROPE_THETA = 500000.0
ROPE_DIM = HEAD_DIM // 4
IDX_ROPE_DIM = IDX_DIM // 4
MAX_TOPK = 256
D_FF = 5632
CONV_WIDTH = 3
EPS = 1e-6

SB_W = N_SB_HEADS * HEAD_DIM
DSA_W = N_DSA_HEADS * HEAD_DIM
IDX_QW = N_IDX_HEADS * IDX_DIM

LANES = 128
VMEM_LIMIT = 56 * 1024 * 1024
INT_MIN = -(2 ** 31)
NEG_BIG = -1e30
EXP_UNDERFLOW = -105.0
LOG2_E = 1.4426950408889634

F32 = jnp.float32
BF16 = jnp.bfloat16


def _params(semantics):
    return pltpu.CompilerParams(dimension_semantics=semantics, vmem_limit_bytes=VMEM_LIMIT)


def _dot(a, b):
    return jnp.dot(a, b, preferred_element_type=F32)


def _dot_nt(a, b):
    return lax.dot_general(a, b, (((1,), (1,)), ((), ())), preferred_element_type=F32)


def _rmsnorm_kernel(x_ref, g_ref, o_ref):
    x = x_ref[...]
    ms = jnp.mean(x * x, axis=-1, keepdims=True)
    o_ref[...] = (x * lax.rsqrt(ms + EPS) * g_ref[...]).astype(o_ref.dtype)


def _rmsnorm(x, g, tm=512):
    T, D = x.shape
    return pl.pallas_call(
        _rmsnorm_kernel,
        out_shape=jax.ShapeDtypeStruct((T, D), BF16),
        grid=(T // tm,),
        in_specs=[pl.BlockSpec((tm, D), lambda i: (i, 0)),
                  pl.BlockSpec((1, D), lambda i: (0, 0))],
        out_specs=pl.BlockSpec((tm, D), lambda i: (i, 0)),
        compiler_params=_params(("parallel",)),
        name="rmsnorm",
    )(x, g.reshape(1, D))


def _rope_head(xh, cos_t, sin_up, sin_dn, half):
    return (xh * cos_t + pltpu.roll(xh, half, 1) * sin_up
            + pltpu.roll(xh, LANES - half, 1) * sin_dn)


def _proj_attn_kernel(a_ref, w_ref, tab_ref, qg_ref, kg_ref, o_ref, *, tn):
    acc = _dot(a_ref[...], w_ref[...])
    region = pl.program_id(1) // (SB_W // tn)
    heads = tn // LANES

    def norm_rope(g_ref):
        for h in range(heads):
            xh = acc[:, h * LANES:(h + 1) * LANES]
            ms = jnp.mean(xh * xh, axis=-1, keepdims=True)
            xh = xh * lax.rsqrt(ms + EPS) * g_ref[...]
            out = _rope_head(xh, tab_ref[:, 0:LANES], tab_ref[:, LANES:2 * LANES],
                             tab_ref[:, 2 * LANES:3 * LANES], ROPE_DIM // 2)
            o_ref[:, h * LANES:(h + 1) * LANES] = out.astype(o_ref.dtype)

    @pl.when((region <= 2) | (region == 5))
    def _():
        o_ref[...] = acc.astype(o_ref.dtype)

    @pl.when(region == 3)
    def _():
        norm_rope(qg_ref)

    @pl.when(region == 4)
    def _():
        norm_rope(kg_ref)

    @pl.when(region == 6)
    def _():
        for h in range(heads):
            xh = acc[:, h * LANES:(h + 1) * LANES]
            out = _rope_head(xh, tab_ref[:, 3 * LANES:4 * LANES], tab_ref[:, 4 * LANES:5 * LANES],
                             tab_ref[:, 5 * LANES:6 * LANES], IDX_ROPE_DIM // 2)
            o_ref[:, h * LANES:(h + 1) * LANES] = out.astype(o_ref.dtype)


def _proj_attn(h, w, tab, qg, kg, tm=512, tn=512):
    T, K = h.shape
    N = w.shape[1]
    return pl.pallas_call(
        functools.partial(_proj_attn_kernel, tn=tn),
        out_shape=jax.ShapeDtypeStruct((T, N), BF16),
        grid=(T // tm, N // tn),
        in_specs=[pl.BlockSpec((tm, K), lambda i, j: (i, 0)),
                  pl.BlockSpec((K, tn), lambda i, j: (0, j)),
                  pl.BlockSpec((tm, 6 * LANES), lambda i, j: (i, 0)),
                  pl.BlockSpec((1, LANES), lambda i, j: (0, 0)),
                  pl.BlockSpec((1, LANES), lambda i, j: (0, 0))],
        out_specs=pl.BlockSpec((tm, tn), lambda i, j: (i, j)),
        compiler_params=_params(("parallel", "arbitrary")),
        name="proj_attn",
    )(h, w, tab, qg.reshape(1, LANES), kg.reshape(1, LANES))


def _matmul_kernel(a_ref, w_ref, o_ref):
    o_ref[...] = _dot(a_ref[...], w_ref[...]).astype(o_ref.dtype)


def _matmul(a, w, out_dtype, tm=512, tn=512, name="matmul"):
    T, K = a.shape
    N = w.shape[1]
    return pl.pallas_call(
        _matmul_kernel,
        out_shape=jax.ShapeDtypeStruct((T, N), out_dtype),
        grid=(T // tm, N // tn),
        in_specs=[pl.BlockSpec((tm, K), lambda i, j: (i, 0)),
                  pl.BlockSpec((K, tn), lambda i, j: (0, j))],
        out_specs=pl.BlockSpec((tm, tn), lambda i, j: (i, j)),
        compiler_params=_params(("parallel", "arbitrary")),
        name=name,
    )(a, w)


def _proj_idx_kernel(a_ref, w_ref, tab_ref, ik_ref, iw_ref):
    acc = _dot(a_ref[...], w_ref[...])
    kk = acc[:, 0:LANES]
    out = _rope_head(kk, tab_ref[:, 3 * LANES:4 * LANES], tab_ref[:, 4 * LANES:5 * LANES],
                     tab_ref[:, 5 * LANES:6 * LANES], IDX_ROPE_DIM // 2)
    ik_ref[...] = out.astype(ik_ref.dtype)
    iw_ref[...] = acc[:, LANES:2 * LANES]


def _proj_idx(h, w, tab, tm=512):
    T, K = h.shape
    return pl.pallas_call(
        _proj_idx_kernel,
        out_shape=(jax.ShapeDtypeStruct((T, LANES), BF16), jax.ShapeDtypeStruct((T, LANES), F32)),
        grid=(T // tm,),
        in_specs=[pl.BlockSpec((tm, K), lambda i: (i, 0)),
                  pl.BlockSpec((K, 2 * LANES), lambda i: (0, 0)),
                  pl.BlockSpec((tm, 6 * LANES), lambda i: (i, 0))],
        out_specs=(pl.BlockSpec((tm, LANES), lambda i: (i, 0)),
                   pl.BlockSpec((tm, LANES), lambda i: (i, 0))),
        compiler_params=_params(("parallel",)),
        name="proj_idx",
    )(h, w, tab)


def _sb_kernel(q_ref, k_ref, v_ref, o_ref, acc_ref, carry_ref, *, t, nh, scale):
    qi = pl.program_id(2)
    tri = (lax.broadcasted_iota(jnp.int32, (t, t), 0)
           > lax.broadcasted_iota(jnp.int32, (t, t), 1)).astype(BF16)
    t_idx = qi * t + lax.broadcasted_iota(jnp.int32, (t, t), 0)
    col = lax.broadcasted_iota(jnp.int32, (t, t), 1)

    def block(h, off, carry):
        hs = slice(h * HEAD_DIM, (h + 1) * HEAD_DIM)
        z = _dot_nt(q_ref[:, hs], k_ref[pl.ds(off, t), hs]) * scale
        mask = (off + col) < t_idx
        e = jnp.exp(-jnp.abs(z))
        log_beta = jnp.minimum(z, 0.0) - jnp.log(1.0 + e)
        l1 = jnp.where(mask, log_beta - z, 0.0)
        hi = l1.astype(BF16)
        lo = (l1 - hi.astype(F32)).astype(BF16)
        suffix = _dot(hi, tri) + _dot(lo, tri)
        a = jnp.where(mask, jnp.exp(log_beta + suffix + carry), 0.0)
        return _dot(a.astype(BF16), v_ref[pl.ds(off, t), hs]), jnp.sum(l1, axis=1, keepdims=True)

    first = jnp.maximum(qi - 1, 0)
    off0 = pl.multiple_of(first * t, t)
    alive = jnp.int32(0)
    for h in range(nh):
        c_late, r_late = block(h, off0 + t, 0.0)
        c_early, r_early = block(h, off0, r_late)
        acc_ref[h] = c_late + c_early
        carry = r_late + r_early
        carry_ref[h] = carry
        alive = jnp.maximum(alive, (jnp.max(carry) > EXP_UNDERFLOW).astype(jnp.int32))

    def body(state):
        j, _ = state
        off = pl.multiple_of(j * t, t)
        alive = jnp.int32(0)
        for h in range(nh):
            carry = carry_ref[h]
            c, r = block(h, off, carry)
            acc_ref[h] += c
            carry = carry + r
            carry_ref[h] = carry
            alive = jnp.maximum(alive, (jnp.max(carry) > EXP_UNDERFLOW).astype(jnp.int32))
        return j - 1, alive

    lax.while_loop(lambda s: (s[0] >= 0) & (s[1] > 0), body, (first - 1, alive))
    for h in range(nh):
        o_ref[:, h * HEAD_DIM:(h + 1) * HEAD_DIM] = acc_ref[h].astype(o_ref.dtype)


def _sb_attention(pa, B, S, t=256, nh=2):
    G = N_SB_HEADS // nh
    w = nh * HEAD_DIM
    return pl.pallas_call(
        functools.partial(_sb_kernel, t=t, nh=nh, scale=HEAD_DIM ** -0.5),
        out_shape=jax.ShapeDtypeStruct((B, S, SB_W), BF16),
        grid=(B, G, S // t),
        in_specs=[pl.BlockSpec((None, t, w), lambda b, g, i: (b, i, g)),
                  pl.BlockSpec((None, S, w), lambda b, g, i: (b, 0, G + g)),
                  pl.BlockSpec((None, S, w), lambda b, g, i: (b, 0, 2 * G + g))],
        out_specs=pl.BlockSpec((None, t, w), lambda b, g, i: (b, i, g)),
        scratch_shapes=[pltpu.VMEM((nh, t, HEAD_DIM), F32), pltpu.VMEM((nh, t, 1), F32)],
        compiler_params=_params(("parallel", "parallel", "arbitrary")),
        name="sb_attention",
    )(pa, pa, pa)


def _indexer_kernel(iq_ref, iw_ref, ik_ref, mask_ref, lhs_ref, key_ref, *, tq, tk, topk, n_kc, idx_scale):
    qi = pl.program_id(1)
    n_causal = ((qi + 1) * tq + tk - 1) // tk
    low = lax.broadcasted_iota(jnp.int32, (tq, LANES), 1) < IDX_DIM
    for p in range(N_IDX_HEADS // 2):
        pair = iq_ref[:, p * LANES:(p + 1) * LANES]
        lhs_ref[2 * p] = jnp.where(low, pair, jnp.zeros_like(pair))
        lhs_ref[2 * p + 1] = jnp.where(low, jnp.zeros_like(pair), pair)
    w = iw_ref[...] * idx_scale
    t_col = qi * tq + lax.broadcasted_iota(jnp.int32, (tq, 1), 0)
    t_idx = qi * tq + lax.broadcasted_iota(jnp.int32, (tq, tk), 0)
    col = lax.broadcasted_iota(jnp.int32, (tq, tk), 1)

    def score_chunk(c, _):
        off = pl.multiple_of(c * tk, tk)
        kb = ik_ref[pl.ds(off, tk), :]
        score = jnp.zeros((tq, tk), F32)
        for h in range(N_IDX_HEADS):
            sc = _dot_nt(lhs_ref[h], kb)
            score = score + jnp.maximum(sc, 0.0) * w[:, h:h + 1]
        bits = pltpu.bitcast(score, jnp.int32)
        key = bits ^ ((bits >> 31) & jnp.int32(0x7FFFFFFF))
        key_ref[c] = jnp.where((off + col) <= t_idx, key, jnp.int32(INT_MIN))
        return 0

    lax.fori_loop(0, n_causal, score_chunk, 0)

    keff = jnp.minimum(t_col + 1, topk)

    def count_ge(cand):
        def body(c, acc):
            k = key_ref[c]
            for l in range(tk // LANES):
                acc = acc + (k[:, l * LANES:(l + 1) * LANES] >= cand).astype(jnp.int32)
            return acc
        acc = lax.fori_loop(0, n_causal, body, jnp.zeros((tq, LANES), jnp.int32))
        return jnp.sum(acc.astype(F32), axis=1, keepdims=True).astype(jnp.int32)

    zero = jnp.zeros((tq, 1), jnp.int32)
    ans = jnp.where(count_ge(zero) >= keff, zero, jnp.int32(INT_MIN))

    def bit_step(i, ans):
        cand = ans | (jnp.int32(1) << (30 - i))
        return jnp.where(count_ge(cand) >= keff, cand, ans)

    thr = lax.fori_loop(0, 31, bit_step, ans)

    def write_chunk(c, _):
        off = pl.multiple_of(c * tk, tk)
        sel = (key_ref[c] >= thr) & ((off + col) <= t_idx)
        mask_ref[c] = jnp.where(sel, 1, 0).astype(mask_ref.dtype)
        return 0

    lax.fori_loop(0, n_causal, write_chunk, 0)

    def zero_chunk(c, _):
        mask_ref[c] = jnp.zeros((tq, tk), mask_ref.dtype)
        return 0

    lax.fori_loop(n_causal, n_kc, zero_chunk, 0)


def _indexer(pa, ik2, iw, B, S, topk, tq=256, tk=512):
    n_kc = S // tk
    return pl.pallas_call(
        functools.partial(_indexer_kernel, tq=tq, tk=tk, topk=topk, n_kc=n_kc,
                          idx_scale=IDX_DIM ** -0.5 * N_IDX_HEADS ** -0.5),
        out_shape=jax.ShapeDtypeStruct((B, n_kc, S, tk), jnp.int8),
        grid=(B, S // tq),
        in_specs=[pl.BlockSpec((None, tq, IDX_QW), lambda b, i: (b, i, 6)),
                  pl.BlockSpec((None, tq, LANES), lambda b, i: (b, i, 0)),
                  pl.BlockSpec((None, S, LANES), lambda b, i: (b, 0, 0))],
        out_specs=pl.BlockSpec((None, n_kc, tq, tk), lambda b, i: (b, 0, i, 0)),
        scratch_shapes=[pltpu.VMEM((N_IDX_HEADS, tq, LANES), BF16),
                        pltpu.VMEM((n_kc, tq, tk), jnp.int32)],
        compiler_params=_params(("parallel", "arbitrary")),
        name="indexer",
    )(pa, iw, ik2)


def _dsa_kernel(q_ref, k_ref, v_ref, m_ref, o_ref, acc_ref, mx_ref, l_ref, bias_ref, *, t, scale):
    qi = pl.program_id(1)
    kj = pl.program_id(2)
    reps = t // LANES

    @pl.when(kj == 0)
    def _():
        acc_ref[...] = jnp.zeros_like(acc_ref)
        mx_ref[...] = jnp.full_like(mx_ref, NEG_BIG)
        l_ref[...] = jnp.zeros_like(l_ref)

    @pl.when(kj <= qi)
    def _():
        bias_ref[...] = jnp.where(m_ref[...].astype(F32) != 0.0, 0.0, -jnp.inf)
        for h in range(N_DSA_HEADS):
            hs = slice(h * HEAD_DIM, (h + 1) * HEAD_DIM)
            s = _dot_nt(q_ref[:, hs], k_ref[:, hs]) * (scale * LOG2_E) + bias_ref[...]
            m_old = mx_ref[h]
            m_new = jnp.maximum(m_old, jnp.max(s, axis=1, keepdims=True))
            alpha = jnp.exp2(m_old - m_new)
            p = jnp.exp2(s - jnp.tile(m_new, (1, reps)))
            l_ref[h] = alpha * l_ref[h] + jnp.sum(p, axis=1, keepdims=True)
            acc_ref[:, hs] = alpha * acc_ref[:, hs] + _dot(p.astype(BF16), v_ref[:, hs])
            mx_ref[h] = m_new

    @pl.when(kj == pl.num_programs(2) - 1)
    def _():
        for h in range(N_DSA_HEADS):
            hs = slice(h * HEAD_DIM, (h + 1) * HEAD_DIM)
            o_ref[:, hs] = (acc_ref[:, hs] / l_ref[h]).astype(o_ref.dtype)


def _dsa_attention(pa, mask, B, S, t=512):
    n = S // t
    return pl.pallas_call(
        functools.partial(_dsa_kernel, t=t, scale=HEAD_DIM ** -0.5),
        out_shape=jax.ShapeDtypeStruct((B, S, DSA_W), BF16),
        grid=(B, n, n),
        in_specs=[pl.BlockSpec((None, t, DSA_W), lambda b, i, j: (b, i, 3)),
                  pl.BlockSpec((None, t, DSA_W), lambda b, i, j: (b, jnp.minimum(j, i), 4)),
                  pl.BlockSpec((None, t, DSA_W), lambda b, i, j: (b, jnp.minimum(j, i), 5)),
                  pl.BlockSpec((None, None, t, t), lambda b, i, j: (b, jnp.minimum(j, i), i, 0))],
        out_specs=pl.BlockSpec((None, t, DSA_W), lambda b, i, j: (b, i, 0)),
        scratch_shapes=[pltpu.VMEM((t, DSA_W), F32),
                        pltpu.VMEM((N_DSA_HEADS, t, LANES), F32),
                        pltpu.VMEM((N_DSA_HEADS, t, LANES), F32),
                        pltpu.VMEM((t, t), F32)],
        compiler_params=_params(("parallel", "parallel", "arbitrary")),
        name="dsa_attention",
    )(pa, pa, pa, mask)


def _merge_kernel(osb_ref, ods_ref, wsb_ref, wds_ref, gsb_ref, gds_ref, o_ref):
    a = _dot(osb_ref[...], wsb_ref[...])
    b = _dot(ods_ref[...], wds_ref[...])
    o_ref[...] = (jax.nn.sigmoid(gsb_ref[...]) * a + jax.nn.sigmoid(gds_ref[...]) * b).astype(o_ref.dtype)


def _merge(o_sb, o_ds, w_branch, gates, tm=512, tn=512):
    T = o_sb.shape[0]
    nj = D_MODEL // tn
    return pl.pallas_call(
        _merge_kernel,
        out_shape=jax.ShapeDtypeStruct((T, D_MODEL), BF16),
        grid=(T // tm, nj),
        in_specs=[pl.BlockSpec((tm, SB_W), lambda i, j: (i, 0)),
                  pl.BlockSpec((tm, DSA_W), lambda i, j: (i, 0)),
                  pl.BlockSpec((SB_W, tn), lambda i, j: (0, j)),
                  pl.BlockSpec((DSA_W, tn), lambda i, j: (SB_W // DSA_W, j)),
                  pl.BlockSpec((tm, tn), lambda i, j: (i, j)),
                  pl.BlockSpec((tm, tn), lambda i, j: (i, nj + j))],
        out_specs=pl.BlockSpec((tm, tn), lambda i, j: (i, j)),
        compiler_params=_params(("parallel", "arbitrary")),
        name="merge",
    )(o_sb, o_ds, w_branch, w_branch, gates, gates)


def _matmul_res_kernel(a_ref, w_ref, r_ref, o_ref):
    o_ref[...] = r_ref[...] + _dot(a_ref[...], w_ref[...])


def _matmul_res(a, w, res, tm=512, tn=512, name="matmul_res"):
    T, K = a.shape
    N = w.shape[1]
    return pl.pallas_call(
        _matmul_res_kernel,
        out_shape=jax.ShapeDtypeStruct((T, N), F32),
        grid=(T // tm, N // tn),
        in_specs=[pl.BlockSpec((tm, K), lambda i, j: (i, 0)),
                  pl.BlockSpec((K, tn), lambda i, j: (0, j)),
                  pl.BlockSpec((tm, tn), lambda i, j: (i, j))],
        out_specs=pl.BlockSpec((tm, tn), lambda i, j: (i, j)),
        compiler_params=_params(("parallel", "arbitrary")),
        name=name,
    )(a, w, res)


def _conv_gate_kernel(ug_ref, uv_ref, pg_ref, pv_ref, wg_ref, wv_ref, bg_ref, bv_ref, o_ref, *, tm, rows_per_seq):
    first = (pl.program_id(0) % (rows_per_seq // tm)) == 0
    row = lax.broadcasted_iota(jnp.int32, ug_ref.shape, 0)

    def conv(u_ref, p_ref, w_ref, b_ref):
        u = u_ref[...]
        prev = jnp.where(first, 0.0, p_ref[...])
        p1 = prev[7:8, :]
        p2 = prev[6:7, :]
        u1 = jnp.where(row == 0, p1, pltpu.roll(u, 1, 0))
        u2 = jnp.where(row == 0, p2, jnp.where(row == 1, p1, pltpu.roll(u, 2, 0)))
        return w_ref[0:1, :] * u2 + w_ref[1:2, :] * u1 + w_ref[2:3, :] * u + b_ref[...]

    g = conv(ug_ref, pg_ref, wg_ref, bg_ref)
    v = conv(uv_ref, pv_ref, wv_ref, bv_ref)
    o_ref[...] = (g * jax.nn.sigmoid(g) * v).astype(o_ref.dtype)


def _conv_gate(u, conv_w, conv_b, rows_per_seq, tm=512, tc=512):
    T = u.shape[0]
    nj = D_FF // tc
    sub = 8
    prev_map_g = lambda i, j: (jnp.maximum(i * (tm // sub) - 1, 0), j)
    prev_map_v = lambda i, j: (jnp.maximum(i * (tm // sub) - 1, 0), nj + j)
    cb = conv_b.reshape(1, 2 * D_FF)
    return pl.pallas_call(
        functools.partial(_conv_gate_kernel, tm=tm, rows_per_seq=rows_per_seq),
        out_shape=jax.ShapeDtypeStruct((T, D_FF), BF16),
        grid=(T // tm, nj),
        in_specs=[pl.BlockSpec((tm, tc), lambda i, j: (i, j)),
                  pl.BlockSpec((tm, tc), lambda i, j: (i, nj + j)),
                  pl.BlockSpec((sub, tc), prev_map_g),
                  pl.BlockSpec((sub, tc), prev_map_v),
                  pl.BlockSpec((CONV_WIDTH, tc), lambda i, j: (0, j)),
                  pl.BlockSpec((CONV_WIDTH, tc), lambda i, j: (0, nj + j)),
                  pl.BlockSpec((1, tc), lambda i, j: (0, j)),
                  pl.BlockSpec((1, tc), lambda i, j: (0, nj + j))],
        out_specs=pl.BlockSpec((tm, tc), lambda i, j: (i, j)),
        compiler_params=_params(("parallel", "parallel")),
        name="conv_gate",
    )(u, u, u, u, conv_w, conv_w, cb, cb)


def _rope_tables(positions):
    pos = positions.reshape(-1).astype(F32)[:, None]

    def tables(rot, width):
        half = rot // 2
        freqs = ROPE_THETA ** (-jnp.arange(0, rot, 2, dtype=F32) / rot)
        ang = pos * freqs
        cos, sin = jnp.cos(ang), jnp.sin(ang)
        T = pos.shape[0]
        ones = jnp.ones((T, width - rot), F32)
        zeros = jnp.zeros((T, width - rot), F32)
        zh = jnp.zeros((T, half), F32)
        c = jnp.concatenate([cos, cos, ones], axis=1)
        up = jnp.concatenate([zh, sin, zeros], axis=1)
        dn = jnp.concatenate([-sin, zh, zeros], axis=1)
        rep = LANES // width
        return [jnp.tile(a, (1, rep)) for a in (c, up, dn)]

    return jnp.concatenate(tables(ROPE_DIM, HEAD_DIM) + tables(IDX_ROPE_DIM, IDX_DIM), axis=1)


def kernel(x, positions, attn_norm_g, w_in, q_norm_g, k_norm_g, w_branch, w_out, ffn_norm_g, w_up,
           conv_w, conv_b, w_down):
    B, S, D = x.shape
    assert D == D_MODEL and S % 512 == 0
    T = B * S
    topk = min(MAX_TOPK, S // 4)
    xf = x.reshape(T, D)

    o_qix = 3 * SB_W + 3 * DSA_W
    o_kix = o_qix + IDX_QW
    o_wix = o_kix + IDX_DIM
    o_g = o_wix + N_IDX_HEADS
    w_attn = w_in[:, :o_kix].astype(BF16)
    w_kix = w_in[:, o_kix:o_wix]
    w_idx = jnp.concatenate([w_kix, w_kix, w_in[:, o_wix:o_g],
                             jnp.zeros((D, LANES - N_IDX_HEADS), w_in.dtype)], axis=1).astype(BF16)
    w_gate = w_in[:, o_g:].astype(BF16)

    tab = _rope_tables(positions)

    h1 = _rmsnorm(xf, attn_norm_g)
    pa = _proj_attn(h1, w_attn, tab, q_norm_g, k_norm_g)
    gates = _matmul(h1, w_gate, F32, name="proj_gates")
    ik2, iw = _proj_idx(h1, w_idx, tab)

    pa3 = pa.reshape(B, S, pa.shape[1])
    o_sb = _sb_attention(pa3, B, S)
    mask = _indexer(pa3, ik2.reshape(B, S, LANES), iw.reshape(B, S, LANES), B, S, topk)
    o_ds = _dsa_attention(pa3, mask, B, S)

    merged = _merge(o_sb.reshape(T, SB_W), o_ds.reshape(T, DSA_W), w_branch.astype(BF16), gates)
    x1 = _matmul_res(merged, w_out.astype(BF16), xf, name="out_proj")

    h2 = _rmsnorm(x1, ffn_norm_g)
    u = _matmul(h2, w_up.astype(BF16), F32, name="up_proj")
    act = _conv_gate(u, conv_w, conv_b, S)
    out = _matmul_res(act, w_down.astype(BF16), x1, name="down_proj")
    return out.reshape(B, S, D)
```

```python
import functools

import jax
import jax.numpy as jnp
from jax import lax
from jax.experimental import pallas as pl
from jax.experimental.pallas import tpu as pltpu

D_MODEL = 2048
HEAD_DIM = 128
N_SB_HEADS = 8
N_DSA_HEADS = 8
N_IDX_HEADS = 16
IDX_DIM = 64
ROPE_THETA = 500000.0
ROPE_DIM = HEAD_DIM // 4
IDX_ROPE_DIM = IDX_DIM // 4
MAX_TOPK = 256
D_FF = 5632
CONV_WIDTH = 3
EPS = 1e-6

SB_W = N_SB_HEADS * HEAD_DIM
DSA_W = N_DSA_HEADS * HEAD_DIM
IDX_QW = N_IDX_HEADS * IDX_DIM

LANES = 128
VMEM_LIMIT = 56 * 1024 * 1024
NEG_BIG = -1e30
EXP_UNDERFLOW = -105.0
LOG2_E = 1.4426950408889634

F32 = jnp.float32
BF16 = jnp.bfloat16


def _params(semantics):
    return pltpu.CompilerParams(dimension_semantics=semantics, vmem_limit_bytes=VMEM_LIMIT)


def _dot(a, b):
    return jnp.dot(a, b, preferred_element_type=F32)


def _dot_nt(a, b):
    return lax.dot_general(a, b, (((1,), (1,)), ((), ())), preferred_element_type=F32)


def _rmsnorm_kernel(x_ref, g_ref, o_ref):
    x = x_ref[...]
    ms = jnp.mean(x * x, axis=-1, keepdims=True)
    o_ref[...] = (x * lax.rsqrt(ms + EPS) * g_ref[...]).astype(o_ref.dtype)


def _rmsnorm(x, g, tm=512):
    T, D = x.shape
    return pl.pallas_call(
        _rmsnorm_kernel,
        out_shape=jax.ShapeDtypeStruct((T, D), BF16),
        grid=(T // tm,),
        in_specs=[pl.BlockSpec((tm, D), lambda i: (i, 0)),
                  pl.BlockSpec((1, D), lambda i: (0, 0))],
        out_specs=pl.BlockSpec((tm, D), lambda i: (i, 0)),
        compiler_params=_params(("parallel",)),
        name="rmsnorm",
    )(x, g.reshape(1, D))


def _rope_head(xh, cos_t, sin_up, sin_dn, half):
    return (xh * cos_t + pltpu.roll(xh, half, 1) * sin_up
            + pltpu.roll(xh, LANES - half, 1) * sin_dn)


def _proj_attn_kernel(a_ref, w_ref, tab_ref, qg_ref, kg_ref, o_ref, *, tn):
    acc = _dot(a_ref[...], w_ref[...])
    region = pl.program_id(1) // (SB_W // tn)
    heads = tn // LANES

    def norm_rope(g_ref):
        for h in range(heads):
            xh = acc[:, h * LANES:(h + 1) * LANES]
            ms = jnp.mean(xh * xh, axis=-1, keepdims=True)
            xh = xh * lax.rsqrt(ms + EPS) * g_ref[...]
            out = _rope_head(xh, tab_ref[:, 0:LANES], tab_ref[:, LANES:2 * LANES],
                             tab_ref[:, 2 * LANES:3 * LANES], ROPE_DIM // 2)
            o_ref[:, h * LANES:(h + 1) * LANES] = out.astype(o_ref.dtype)

    @pl.when((region <= 2) | (region == 5))
    def _():
        o_ref[...] = acc.astype(o_ref.dtype)

    @pl.when(region == 3)
    def _():
        norm_rope(qg_ref)

    @pl.when(region == 4)
    def _():
        norm_rope(kg_ref)

    @pl.when(region == 6)
    def _():
        for h in range(heads):
            xh = acc[:, h * LANES:(h + 1) * LANES]
            out = _rope_head(xh, tab_ref[:, 3 * LANES:4 * LANES], tab_ref[:, 4 * LANES:5 * LANES],
                             tab_ref[:, 5 * LANES:6 * LANES], IDX_ROPE_DIM // 2)
            o_ref[:, h * LANES:(h + 1) * LANES] = out.astype(o_ref.dtype)


def _proj_attn(h, w, tab, qg, kg, tm=1024, tn=1024):
    T, K = h.shape
    N = w.shape[1]
    return pl.pallas_call(
        functools.partial(_proj_attn_kernel, tn=tn),
        out_shape=jax.ShapeDtypeStruct((T, N), BF16),
        grid=(T // tm, N // tn),
        in_specs=[pl.BlockSpec((tm, K), lambda i, j: (i, 0)),
                  pl.BlockSpec((K, tn), lambda i, j: (0, j)),
                  pl.BlockSpec((tm, 6 * LANES), lambda i, j: (i, 0)),
                  pl.BlockSpec((1, LANES), lambda i, j: (0, 0)),
                  pl.BlockSpec((1, LANES), lambda i, j: (0, 0))],
        out_specs=pl.BlockSpec((tm, tn), lambda i, j: (i, j)),
        compiler_params=_params(("parallel", "arbitrary")),
        name="proj_attn",
    )(h, w, tab, qg.reshape(1, LANES), kg.reshape(1, LANES))


def _matmul_kernel(a_ref, w_ref, o_ref):
    o_ref[...] = _dot(a_ref[...], w_ref[...]).astype(o_ref.dtype)


def _matmul(a, w, out_dtype, tm=1024, tn=1024, name="matmul"):
    T, K = a.shape
    N = w.shape[1]
    return pl.pallas_call(
        _matmul_kernel,
        out_shape=jax.ShapeDtypeStruct((T, N), out_dtype),
        grid=(T // tm, N // tn),
        in_specs=[pl.BlockSpec((tm, K), lambda i, j: (i, 0)),
                  pl.BlockSpec((K, tn), lambda i, j: (0, j))],
        out_specs=pl.BlockSpec((tm, tn), lambda i, j: (i, j)),
        compiler_params=_params(("parallel", "arbitrary")),
        name=name,
    )(a, w)


def _proj_idx_kernel(a_ref, w_ref, tab_ref, ik_ref, iw_ref):
    acc = _dot(a_ref[...], w_ref[...])
    kk = acc[:, 0:LANES]
    out = _rope_head(kk, tab_ref[:, 3 * LANES:4 * LANES], tab_ref[:, 4 * LANES:5 * LANES],
                     tab_ref[:, 5 * LANES:6 * LANES], IDX_ROPE_DIM // 2)
    ik_ref[...] = out.astype(ik_ref.dtype)
    iw_ref[...] = acc[:, LANES:2 * LANES]


def _proj_idx(h, w, tab, tm=512):
    T, K = h.shape
    return pl.pallas_call(
        _proj_idx_kernel,
        out_shape=(jax.ShapeDtypeStruct((T, LANES), BF16), jax.ShapeDtypeStruct((T, LANES), F32)),
        grid=(T // tm,),
        in_specs=[pl.BlockSpec((tm, K), lambda i: (i, 0)),
                  pl.BlockSpec((K, 2 * LANES), lambda i: (0, 0)),
                  pl.BlockSpec((tm, 6 * LANES), lambda i: (i, 0))],
        out_specs=(pl.BlockSpec((tm, LANES), lambda i: (i, 0)),
                   pl.BlockSpec((tm, LANES), lambda i: (i, 0))),
        compiler_params=_params(("parallel",)),
        name="proj_idx",
    )(h, w, tab)


def _sb_kernel(q_ref, k_ref, v_ref, o_ref, acc_ref, carry_ref, *, t, nh, scale):
    qi = pl.program_id(2)
    tri = (lax.broadcasted_iota(jnp.int32, (t, t), 0)
           > lax.broadcasted_iota(jnp.int32, (t, t), 1)).astype(BF16)
    t_idx = qi * t + lax.broadcasted_iota(jnp.int32, (t, t), 0)
    col = lax.broadcasted_iota(jnp.int32, (t, t), 1)

    def block(h, off, carry):
        hs = slice(h * HEAD_DIM, (h + 1) * HEAD_DIM)
        z = _dot_nt(q_ref[:, hs], k_ref[pl.ds(off, t), hs]) * scale
        mask = (off + col) < t_idx
        e = jnp.exp(-jnp.abs(z))
        log_beta = jnp.minimum(z, 0.0) - jnp.log(1.0 + e)
        l1 = jnp.where(mask, log_beta - z, 0.0)
        hi = l1.astype(BF16)
        lo = (l1 - hi.astype(F32)).astype(BF16)
        suffix = _dot(hi, tri) + _dot(lo, tri)
        a = jnp.where(mask, jnp.exp(log_beta + suffix + carry), 0.0)
        return _dot(a.astype(BF16), v_ref[pl.ds(off, t), hs]), jnp.sum(l1, axis=1, keepdims=True)

    first = jnp.maximum(qi - 1, 0)
    off0 = pl.multiple_of(first * t, t)
    alive = jnp.int32(0)
    for h in range(nh):
        c_late, r_late = block(h, off0 + t, 0.0)
        c_early, r_early = block(h, off0, r_late)
        acc_ref[h] = c_late + c_early
        carry = r_late + r_early
        carry_ref[h] = carry
        alive = jnp.maximum(alive, (jnp.max(carry) > EXP_UNDERFLOW).astype(jnp.int32))

    def body(state):
        j, _ = state
        off = pl.multiple_of(j * t, t)
        alive = jnp.int32(0)
        for h in range(nh):
            carry = carry_ref[h]
            c, r = block(h, off, carry)
            acc_ref[h] += c
            carry = carry + r
            carry_ref[h] = carry
            alive = jnp.maximum(alive, (jnp.max(carry) > EXP_UNDERFLOW).astype(jnp.int32))
        return j - 1, alive

    lax.while_loop(lambda s: (s[0] >= 0) & (s[1] > 0), body, (first - 1, alive))
    for h in range(nh):
        o_ref[:, h * HEAD_DIM:(h + 1) * HEAD_DIM] = acc_ref[h].astype(o_ref.dtype)


def _sb_attention(pa, B, S, t=256, nh=2):
    G = N_SB_HEADS // nh
    w = nh * HEAD_DIM
    return pl.pallas_call(
        functools.partial(_sb_kernel, t=t, nh=nh, scale=HEAD_DIM ** -0.5),
        out_shape=jax.ShapeDtypeStruct((B, S, SB_W), BF16),
        grid=(B, G, S // t),
        in_specs=[pl.BlockSpec((None, t, w), lambda b, g, i: (b, i, g)),
                  pl.BlockSpec((None, S, w), lambda b, g, i: (b, 0, G + g)),
                  pl.BlockSpec((None, S, w), lambda b, g, i: (b, 0, 2 * G + g))],
        out_specs=pl.BlockSpec((None, t, w), lambda b, g, i: (b, i, g)),
        scratch_shapes=[pltpu.VMEM((nh, t, HEAD_DIM), F32), pltpu.VMEM((nh, t, 1), F32)],
        compiler_params=_params(("parallel", "parallel", "arbitrary")),
        name="sb_attention",
    )(pa, pa, pa)


PACKED_ROWS = 16


def _indexer_kernel(iqt_ref, iwt_ref, ik_ref, bias_ref, rhs_ref, hi_ref, mid_ref, lo_ref, sel_ref, *,
                    tq, tk, to, topk, n_kc, idx_scale):
    qi = pl.program_id(1)
    n_causal = ((qi + 1) * tq + tk - 1) // tk
    G = PACKED_ROWS
    n_groups = tk // G
    first_half = lax.broadcasted_iota(jnp.int32, (LANES, tq), 0) < IDX_DIM
    for p in range(N_IDX_HEADS // 2):
        pair = iqt_ref[p * LANES:(p + 1) * LANES, :]
        rhs_ref[2 * p] = jnp.where(first_half, pair, jnp.zeros_like(pair))
        rhs_ref[2 * p + 1] = jnp.where(first_half, jnp.zeros_like(pair), pair)
    w = iwt_ref[...] * idx_scale
    q_idx = qi * tq + lax.broadcasted_iota(jnp.int32, (tk, tq), 1)
    k_row = lax.broadcasted_iota(jnp.int32, (tk, tq), 0)

    def score_chunk(c, _):
        off = pl.multiple_of(c * tk, tk)
        kb = ik_ref[pl.ds(off, tk), :]
        score = jnp.zeros((tk, tq), F32)
        for h in range(N_IDX_HEADS):
            score = score + jnp.maximum(_dot(kb, rhs_ref[h]), 0.0) * w[h:h + 1, :]
        bits = pltpu.bitcast(score, jnp.int32)
        neg = bits < 0
        hi = pltpu.bitcast(bits & jnp.int32(-65536), F32)
        mid = (bits >> 8) & 0xFF
        lo = bits & 0xFF
        mid = jnp.where(neg, 255 - mid, mid)
        lo = jnp.where(neg, 255 - lo, lo)
        hi_ref[c] = jnp.where((off + k_row) <= q_idx, hi, -jnp.inf).astype(BF16)
        mid_ref[c] = mid.astype(F32).astype(BF16)
        lo_ref[c] = lo.astype(F32).astype(BF16)
        return 0

    lax.fori_loop(0, n_causal, score_chunk, 0)

    one = jnp.ones((G, tq), BF16)
    zero = jnp.zeros((G, tq), BF16)

    def tile16(row8):
        return jnp.concatenate([row8, row8], axis=0).astype(BF16)

    def count(v_ref, cand, strict=False):
        def body(c, tot):
            acc = zero
            for g in range(n_groups):
                vs = v_ref[c, g * G:(g + 1) * G, :]
                hit = (vs > cand) if strict else (vs >= cand)
                acc = acc + jnp.where(hit, one, zero)
            return tot + acc.astype(F32)
        tot = lax.fori_loop(0, n_causal, body, jnp.zeros((G, tq), F32))
        return jnp.broadcast_to(jnp.sum(tot, axis=0, keepdims=True), (8, tq))

    def key_to_f32(k):
        b16 = jnp.where(k >= 0, k, k ^ 0x7FFF) & 0xFFFF
        return pltpu.bitcast(b16 << 16, F32)

    def kth_int(v_ref, rank):
        def step(i, ans):
            cand = ans + (jnp.int32(128) >> i).astype(F32)
            return jnp.where(count(v_ref, tile16(cand)) >= rank, cand, ans)
        return tile16(lax.fori_loop(0, 8, step, jnp.zeros((8, tq), F32)))

    def restrict(src_ref, thr, dst_ref):
        def body(c, _):
            for g in range(n_groups):
                gs = slice(g * G, (g + 1) * G)
                dst_ref[c, gs, :] = jnp.where(src_ref[c, gs, :] == thr, dst_ref[c, gs, :], -one)
            return 0
        lax.fori_loop(0, n_causal, body, 0)

    t_q = qi * tq + lax.broadcasted_iota(jnp.int32, (8, tq), 1)
    keff = jnp.minimum(t_q + 1, topk).astype(F32)

    k0 = jnp.zeros((8, tq), jnp.int32)
    ans = jnp.where(count(hi_ref, tile16(key_to_f32(k0))) >= keff, k0, jnp.int32(-32768))

    def step1(i, ans):
        cand = ans | (jnp.int32(1) << (14 - i))
        return jnp.where(count(hi_ref, tile16(key_to_f32(cand))) >= keff, cand, ans)

    t_hi = tile16(key_to_f32(lax.fori_loop(0, 15, step1, ans)))
    r1 = keff - count(hi_ref, t_hi, strict=True)
    restrict(hi_ref, t_hi, mid_ref)
    t_mid = kth_int(mid_ref, r1)
    r2 = r1 - count(mid_ref, t_mid, strict=True)
    restrict(mid_ref, t_mid, lo_ref)
    t_lo = kth_int(lo_ref, r2)

    eye = (lax.broadcasted_iota(jnp.int32, (tq, tq), 0)
           == lax.broadcasted_iota(jnp.int32, (tq, tq), 1)).astype(BF16)

    def write_chunk(c, _):
        for g in range(n_groups):
            gs = slice(g * G, (g + 1) * G)
            sel = ((hi_ref[c, gs, :] > t_hi) | (mid_ref[c, gs, :] > t_mid) | (lo_ref[c, gs, :] >= t_lo))
            sel_ref[gs, :] = jnp.where(sel, one, zero)
        sel_t = _dot_nt(eye, sel_ref[...])
        bias = jnp.where(sel_t > 0.5, 0.0, -jnp.inf).astype(bias_ref.dtype)
        for s in range(tk // to):
            bias_ref[c * (tk // to) + s] = bias[:, s * to:(s + 1) * to]
        return 0

    lax.fori_loop(0, n_causal, write_chunk, 0)

    def fill_chunk(c, _):
        bias_ref[c] = jnp.full((tq, to), -jnp.inf, bias_ref.dtype)
        return 0

    lax.fori_loop(n_causal * (tk // to), n_kc * (tk // to), fill_chunk, 0)


def _indexer(iqt, iwt, ik2, B, S, topk, tq=256, tk=1024, to=512):
    n_kc = S // tk
    assert tk // PACKED_ROWS <= 256
    field = pltpu.VMEM((n_kc, tk, tq), BF16)
    return pl.pallas_call(
        functools.partial(_indexer_kernel, tq=tq, tk=tk, to=to, topk=topk, n_kc=n_kc,
                          idx_scale=IDX_DIM ** -0.5 * N_IDX_HEADS ** -0.5),
        out_shape=jax.ShapeDtypeStruct((B, S // to, S, to), BF16),
        grid=(B, S // tq),
        in_specs=[pl.BlockSpec((None, IDX_QW, tq), lambda b, i: (b, 0, i)),
                  pl.BlockSpec((None, N_IDX_HEADS, tq), lambda b, i: (b, 0, i)),
                  pl.BlockSpec((None, S, LANES), lambda b, i: (b, 0, 0))],
        out_specs=pl.BlockSpec((None, S // to, tq, to), lambda b, i: (b, 0, i, 0)),
        scratch_shapes=[pltpu.VMEM((N_IDX_HEADS, LANES, tq), BF16), field, field, field,
                        pltpu.VMEM((tk, tq), BF16)],
        compiler_params=_params(("parallel", "arbitrary")),
        name="indexer",
    )(iqt, iwt, ik2)


def _dsa_kernel(q_ref, k_ref, v_ref, m_ref, o_ref, acc_ref, mx_ref, l_ref, bias_ref, *, t, scale):
    qi = pl.program_id(1)
    kj = pl.program_id(2)
    reps = t // LANES

    @pl.when(kj == 0)
    def _():
        acc_ref[...] = jnp.zeros_like(acc_ref)
        mx_ref[...] = jnp.full_like(mx_ref, NEG_BIG)
        l_ref[...] = jnp.zeros_like(l_ref)

    @pl.when(kj <= qi)
    def _():
        bias_ref[...] = m_ref[...].astype(F32)
        for h in range(N_DSA_HEADS):
            hs = slice(h * HEAD_DIM, (h + 1) * HEAD_DIM)
            s = _dot_nt(q_ref[:, hs], k_ref[:, hs]) * (scale * LOG2_E) + bias_ref[...]
            m_old = mx_ref[h]
            m_new = jnp.maximum(m_old, jnp.max(s, axis=1, keepdims=True))
            alpha = jnp.exp2(m_old - m_new)
            p = jnp.exp2(s - jnp.tile(m_new, (1, reps)))
            l_ref[h] = alpha * l_ref[h] + jnp.sum(p, axis=1, keepdims=True)
            acc_ref[:, hs] = alpha * acc_ref[:, hs] + _dot(p.astype(BF16), v_ref[:, hs])
            mx_ref[h] = m_new

    @pl.when(kj == pl.num_programs(2) - 1)
    def _():
        for h in range(N_DSA_HEADS):
            hs = slice(h * HEAD_DIM, (h + 1) * HEAD_DIM)
            o_ref[:, hs] = (acc_ref[:, hs] / l_ref[h]).astype(o_ref.dtype)


def _dsa_attention(pa, bias, B, S, t=512):
    n = S // t
    return pl.pallas_call(
        functools.partial(_dsa_kernel, t=t, scale=HEAD_DIM ** -0.5),
        out_shape=jax.ShapeDtypeStruct((B, S, DSA_W), BF16),
        grid=(B, n, n),
        in_specs=[pl.BlockSpec((None, t, DSA_W), lambda b, i, j: (b, i, 3)),
                  pl.BlockSpec((None, t, DSA_W), lambda b, i, j: (b, jnp.minimum(j, i), 4)),
                  pl.BlockSpec((None, t, DSA_W), lambda b, i, j: (b, jnp.minimum(j, i), 5)),
                  pl.BlockSpec((None, None, t, t), lambda b, i, j: (b, jnp.minimum(j, i), i, 0))],
        out_specs=pl.BlockSpec((None, t, DSA_W), lambda b, i, j: (b, i, 0)),
        scratch_shapes=[pltpu.VMEM((t, DSA_W), F32),
                        pltpu.VMEM((N_DSA_HEADS, t, LANES), F32),
                        pltpu.VMEM((N_DSA_HEADS, t, LANES), F32),
                        pltpu.VMEM((t, t), F32)],
        compiler_params=_params(("parallel", "parallel", "arbitrary")),
        name="dsa_attention",
    )(pa, pa, pa, bias)


def _merge_kernel(osb_ref, ods_ref, wsb_ref, wds_ref, gsb_ref, gds_ref, o_ref):
    a = _dot(osb_ref[...], wsb_ref[...])
    b = _dot(ods_ref[...], wds_ref[...])
    o_ref[...] = (jax.nn.sigmoid(gsb_ref[...]) * a + jax.nn.sigmoid(gds_ref[...]) * b).astype(o_ref.dtype)


def _merge(o_sb, o_ds, w_branch, gates, tm=1024, tn=1024):
    T = o_sb.shape[0]
    nj = D_MODEL // tn
    return pl.pallas_call(
        _merge_kernel,
        out_shape=jax.ShapeDtypeStruct((T, D_MODEL), BF16),
        grid=(T // tm, nj),
        in_specs=[pl.BlockSpec((tm, SB_W), lambda i, j: (i, 0)),
                  pl.BlockSpec((tm, DSA_W), lambda i, j: (i, 0)),
                  pl.BlockSpec((SB_W, tn), lambda i, j: (0, j)),
                  pl.BlockSpec((DSA_W, tn), lambda i, j: (SB_W // DSA_W, j)),
                  pl.BlockSpec((tm, tn), lambda i, j: (i, j)),
                  pl.BlockSpec((tm, tn), lambda i, j: (i, nj + j))],
        out_specs=pl.BlockSpec((tm, tn), lambda i, j: (i, j)),
        compiler_params=_params(("parallel", "arbitrary")),
        name="merge",
    )(o_sb, o_ds, w_branch, w_branch, gates, gates)


def _matmul_res_kernel(a_ref, w_ref, r_ref, o_ref):
    o_ref[...] = r_ref[...] + _dot(a_ref[...], w_ref[...])


def _matmul_res(a, w, res, tm=1024, tn=1024, name="matmul_res"):
    T, K = a.shape
    N = w.shape[1]
    return pl.pallas_call(
        _matmul_res_kernel,
        out_shape=jax.ShapeDtypeStruct((T, N), F32),
        grid=(T // tm, N // tn),
        in_specs=[pl.BlockSpec((tm, K), lambda i, j: (i, 0)),
                  pl.BlockSpec((K, tn), lambda i, j: (0, j)),
                  pl.BlockSpec((tm, tn), lambda i, j: (i, j))],
        out_specs=pl.BlockSpec((tm, tn), lambda i, j: (i, j)),
        compiler_params=_params(("parallel", "arbitrary")),
        name=name,
    )(a, w, res)


HALO = PACKED_ROWS


def _ffn_kernel(x_ref, xp_ref, g_ref, wg_ref, wv_ref, cwg_ref, cwv_ref, cbg_ref, cbv_ref, wd_ref, o_ref,
                h_ref, *, tm, rows_per_seq):
    i = pl.program_id(0)
    j = pl.program_id(1)

    def norm(x):
        ms = jnp.mean(x * x, axis=-1, keepdims=True)
        return (x * lax.rsqrt(ms + EPS) * g_ref[...]).astype(BF16)

    @pl.when(j == 0)
    def _():
        x = x_ref[...]
        h_ref[HALO:, :] = norm(x)
        first = (i % (rows_per_seq // tm)) == 0
        hp = norm(xp_ref[...])
        h_ref[0:HALO, :] = jnp.where(first, jnp.zeros_like(hp), hp)
        o_ref[...] = x

    h = h_ref[...]

    def conv(w_ref, cw_ref, cb_ref):
        u = _dot(h, w_ref[...])
        u0 = u[HALO:, :]
        u1 = pltpu.roll(u, 1, 0)[HALO:, :]
        u2 = pltpu.roll(u, 2, 0)[HALO:, :]
        return cw_ref[0:1, :] * u2 + cw_ref[1:2, :] * u1 + cw_ref[2:3, :] * u0 + cb_ref[...]

    gate = conv(wg_ref, cwg_ref, cbg_ref)
    val = conv(wv_ref, cwv_ref, cbv_ref)
    act = (gate * jax.nn.sigmoid(gate) * val).astype(BF16)
    o_ref[...] += _dot(act, wd_ref[...])


def _ffn(x1, g, w_up, conv_w, conv_b, w_down, rows_per_seq, tm=512, tf=512):
    T, D = x1.shape
    nj = D_FF // tf
    cb = conv_b.reshape(1, 2 * D_FF)
    return pl.pallas_call(
        functools.partial(_ffn_kernel, tm=tm, rows_per_seq=rows_per_seq),
        out_shape=jax.ShapeDtypeStruct((T, D), F32),
        grid=(T // tm, nj),
        in_specs=[pl.BlockSpec((tm, D), lambda i, j: (i, 0)),
                  pl.BlockSpec((HALO, D), lambda i, j: (jnp.maximum(i * (tm // HALO) - 1, 0), 0)),
                  pl.BlockSpec((1, D), lambda i, j: (0, 0)),
                  pl.BlockSpec((D, tf), lambda i, j: (0, j)),
                  pl.BlockSpec((D, tf), lambda i, j: (0, nj + j)),
                  pl.BlockSpec((CONV_WIDTH, tf), lambda i, j: (0, j)),
                  pl.BlockSpec((CONV_WIDTH, tf), lambda i, j: (0, nj + j)),
                  pl.BlockSpec((1, tf), lambda i, j: (0, j)),
                  pl.BlockSpec((1, tf), lambda i, j: (0, nj + j)),
                  pl.BlockSpec((tf, D), lambda i, j: (j, 0))],
        out_specs=pl.BlockSpec((tm, D), lambda i, j: (i, 0)),
        scratch_shapes=[pltpu.VMEM((HALO + tm, D), BF16)],
        compiler_params=_params(("parallel", "arbitrary")),
        name="ffn",
    )(x1, x1, g.reshape(1, D), w_up, w_up, conv_w, conv_w, cb, cb, w_down)


def _rope_tables(positions):
    pos = positions.reshape(-1).astype(F32)[:, None]

    def tables(rot, width):
        half = rot // 2
        freqs = ROPE_THETA ** (-jnp.arange(0, rot, 2, dtype=F32) / rot)
        ang = pos * freqs
        cos, sin = jnp.cos(ang), jnp.sin(ang)
        T = pos.shape[0]
        ones = jnp.ones((T, width - rot), F32)
        zeros = jnp.zeros((T, width - rot), F32)
        zh = jnp.zeros((T, half), F32)
        c = jnp.concatenate([cos, cos, ones], axis=1)
        up = jnp.concatenate([zh, sin, zeros], axis=1)
        dn = jnp.concatenate([-sin, zh, zeros], axis=1)
        rep = LANES // width
        return [jnp.tile(a, (1, rep)) for a in (c, up, dn)]

    return jnp.concatenate(tables(ROPE_DIM, HEAD_DIM) + tables(IDX_ROPE_DIM, IDX_DIM), axis=1)


def kernel(x, positions, attn_norm_g, w_in, q_norm_g, k_norm_g, w_branch, w_out, ffn_norm_g, w_up,
           conv_w, conv_b, w_down):
    B, S, D = x.shape
    assert D == D_MODEL and S % 1024 == 0
    T = B * S
    topk = min(MAX_TOPK, S // 4)
    xf = x.reshape(T, D)

    o_qix = 3 * SB_W + 3 * DSA_W
    o_kix = o_qix + IDX_QW
    o_wix = o_kix + IDX_DIM
    o_g = o_wix + N_IDX_HEADS
    w_attn = w_in[:, :o_kix].astype(BF16)
    w_kix = w_in[:, o_kix:o_wix]
    w_idx = jnp.concatenate([w_kix, w_kix, w_in[:, o_wix:o_g],
                             jnp.zeros((D, LANES - N_IDX_HEADS), w_in.dtype)], axis=1).astype(BF16)
    w_gate = w_in[:, o_g:].astype(BF16)

    tab = _rope_tables(positions)

    h1 = _rmsnorm(xf, attn_norm_g)
    pa = _proj_attn(h1, w_attn, tab, q_norm_g, k_norm_g)
    gates = _matmul(h1, w_gate, F32, name="proj_gates")
    ik2, iw = _proj_idx(h1, w_idx, tab)

    pa3 = pa.reshape(B, S, pa.shape[1])
    o_sb = _sb_attention(pa3, B, S)
    iqt = jnp.swapaxes(pa3[:, :, o_qix:], 1, 2)
    iwt = jnp.swapaxes(iw.reshape(B, S, LANES)[:, :, :N_IDX_HEADS], 1, 2)
    bias = _indexer(iqt, iwt, ik2.reshape(B, S, LANES), B, S, topk)
    o_ds = _dsa_attention(pa3, bias, B, S)

    merged = _merge(o_sb.reshape(T, SB_W), o_ds.reshape(T, DSA_W), w_branch.astype(BF16), gates)
    x1 = _matmul_res(merged, w_out.astype(BF16), xf, name="out_proj")

    out = _ffn(x1, ffn_norm_g, w_up.astype(BF16), conv_w, conv_b, w_down.astype(BF16), S)
    return out.reshape(B, S, D)
```

```python
import functools

import jax
import jax.numpy as jnp
import numpy as np
from jax import lax
from jax.experimental import pallas as pl
from jax.experimental.pallas import tpu as pltpu

D_MODEL = 2048
HEAD_DIM = 128
N_SB_HEADS = 8
N_DSA_HEADS = 8
N_IDX_HEADS = 16
IDX_DIM = 64
ROPE_THETA = 500000.0
ROPE_DIM = HEAD_DIM // 4
IDX_ROPE_DIM = IDX_DIM // 4
MAX_TOPK = 256
D_FF = 5632
CONV_WIDTH = 3
EPS = 1e-6

SB_W = N_SB_HEADS * HEAD_DIM
DSA_W = N_DSA_HEADS * HEAD_DIM
IDX_QW = N_IDX_HEADS * IDX_DIM

LANES = 128
VMEM_LIMIT = 56 * 1024 * 1024
NEG_BIG = -1e30
EXP_UNDERFLOW = -105.0
LOG2_E = 1.4426950408889634

F32 = jnp.float32
BF16 = jnp.bfloat16


def _params(semantics):
    return pltpu.CompilerParams(dimension_semantics=semantics, vmem_limit_bytes=VMEM_LIMIT)


def _dot(a, b):
    return jnp.dot(a, b, preferred_element_type=F32)


def _dot_nt(a, b):
    return lax.dot_general(a, b, (((1,), (1,)), ((), ())), preferred_element_type=F32)


def _rmsnorm_kernel(x_ref, g_ref, o_ref):
    x = x_ref[...]
    ms = jnp.mean(x * x, axis=-1, keepdims=True)
    o_ref[...] = (x * lax.rsqrt(ms + EPS) * g_ref[...]).astype(o_ref.dtype)


def _rmsnorm(x, g, tm=512):
    T, D = x.shape
    return pl.pallas_call(
        _rmsnorm_kernel,
        out_shape=jax.ShapeDtypeStruct((T, D), BF16),
        grid=(T // tm,),
        in_specs=[pl.BlockSpec((tm, D), lambda i: (i, 0)),
                  pl.BlockSpec((1, D), lambda i: (0, 0))],
        out_specs=pl.BlockSpec((tm, D), lambda i: (i, 0)),
        compiler_params=_params(("parallel",)),
        name="rmsnorm",
    )(x, g.reshape(1, D))


def _rope_head(xh, cos_t, sin_up, sin_dn, half):
    return (xh * cos_t + pltpu.roll(xh, half, 1) * sin_up
            + pltpu.roll(xh, LANES - half, 1) * sin_dn)


ROT_SRC = LANES


def _expand_tables(cs_ref, e_ref, c_ref, lo, hi):
    e = e_ref[:, lo * LANES:hi * LANES]
    out = c_ref[:, lo * LANES:hi * LANES]
    for p in range(3):
        out = out + _dot(cs_ref[:, p * ROT_SRC:(p + 1) * ROT_SRC], e)
    return out


def _proj_attn_kernel(a_ref, w_ref, cs_ref, e_ref, c_ref, qg_ref, kg_ref, o_ref, *, tn):
    acc = _dot(a_ref[...], w_ref[...])
    region = pl.program_id(1) // (SB_W // tn)
    heads = tn // LANES

    def norm_rope(g_ref):
        tab = _expand_tables(cs_ref, e_ref, c_ref, 0, 3)
        for h in range(heads):
            xh = acc[:, h * LANES:(h + 1) * LANES]
            ms = jnp.mean(xh * xh, axis=-1, keepdims=True)
            xh = xh * lax.rsqrt(ms + EPS) * g_ref[...]
            out = _rope_head(xh, tab[:, 0:LANES], tab[:, LANES:2 * LANES],
                             tab[:, 2 * LANES:3 * LANES], ROPE_DIM // 2)
            o_ref[:, h * LANES:(h + 1) * LANES] = out.astype(o_ref.dtype)

    @pl.when((region <= 2) | (region == 5))
    def _():
        o_ref[...] = acc.astype(o_ref.dtype)

    @pl.when(region == 3)
    def _():
        norm_rope(qg_ref)

    @pl.when(region == 4)
    def _():
        norm_rope(kg_ref)

    @pl.when(region == 6)
    def _():
        tab = _expand_tables(cs_ref, e_ref, c_ref, 3, 6)
        for h in range(heads):
            xh = acc[:, h * LANES:(h + 1) * LANES]
            out = _rope_head(xh, tab[:, 0:LANES], tab[:, LANES:2 * LANES],
                             tab[:, 2 * LANES:3 * LANES], IDX_ROPE_DIM // 2)
            o_ref[:, h * LANES:(h + 1) * LANES] = out.astype(o_ref.dtype)


def _proj_attn(h, w, N, cs, e, c, qg, kg, tm=1024, tn=1024):
    T, K = h.shape
    return pl.pallas_call(
        functools.partial(_proj_attn_kernel, tn=tn),
        out_shape=jax.ShapeDtypeStruct((T, N), BF16),
        grid=(T // tm, N // tn),
        in_specs=[pl.BlockSpec((tm, K), lambda i, j: (i, 0)),
                  pl.BlockSpec((K, tn), lambda i, j: (0, j)),
                  pl.BlockSpec((tm, 3 * ROT_SRC), lambda i, j: (i, 0)),
                  pl.BlockSpec((ROT_SRC, 6 * LANES), lambda i, j: (0, 0)),
                  pl.BlockSpec((1, 6 * LANES), lambda i, j: (0, 0)),
                  pl.BlockSpec((1, LANES), lambda i, j: (0, 0)),
                  pl.BlockSpec((1, LANES), lambda i, j: (0, 0))],
        out_specs=pl.BlockSpec((tm, tn), lambda i, j: (i, j)),
        compiler_params=_params(("parallel", "arbitrary")),
        name="proj_attn",
    )(h, w, cs, e, c, qg.reshape(1, LANES), kg.reshape(1, LANES))


def _matmul_kernel(a_ref, w_ref, o_ref):
    o_ref[...] = _dot(a_ref[...], w_ref[...]).astype(o_ref.dtype)


def _matmul(a, w, out_dtype, tm=1024, tn=1024, name="matmul"):
    T, K = a.shape
    N = w.shape[1]
    return pl.pallas_call(
        _matmul_kernel,
        out_shape=jax.ShapeDtypeStruct((T, N), out_dtype),
        grid=(T // tm, N // tn),
        in_specs=[pl.BlockSpec((tm, K), lambda i, j: (i, 0)),
                  pl.BlockSpec((K, tn), lambda i, j: (0, j))],
        out_specs=pl.BlockSpec((tm, tn), lambda i, j: (i, j)),
        compiler_params=_params(("parallel", "arbitrary")),
        name=name,
    )(a, w)


def _proj_idx_kernel(a_ref, w_ref, cs_ref, e_ref, c_ref, ik_ref, iw_ref):
    acc = _dot(a_ref[...], w_ref[...])
    kk = acc[:, 0:LANES]
    tab = _expand_tables(cs_ref, e_ref, c_ref, 3, 6)
    out = _rope_head(kk, tab[:, 0:LANES], tab[:, LANES:2 * LANES], tab[:, 2 * LANES:3 * LANES],
                     IDX_ROPE_DIM // 2)
    ik_ref[...] = out.astype(ik_ref.dtype)
    iw_ref[...] = acc[:, LANES:2 * LANES]


def _proj_idx(h, w, cs, e, c, tm=512):
    T, K = h.shape
    return pl.pallas_call(
        _proj_idx_kernel,
        out_shape=(jax.ShapeDtypeStruct((T, LANES), BF16), jax.ShapeDtypeStruct((T, LANES), F32)),
        grid=(T // tm,),
        in_specs=[pl.BlockSpec((tm, K), lambda i: (i, 0)),
                  pl.BlockSpec((K, 2 * LANES), lambda i: (0, 0)),
                  pl.BlockSpec((tm, 3 * ROT_SRC), lambda i: (i, 0)),
                  pl.BlockSpec((ROT_SRC, 6 * LANES), lambda i: (0, 0)),
                  pl.BlockSpec((1, 6 * LANES), lambda i: (0, 0))],
        out_specs=(pl.BlockSpec((tm, LANES), lambda i: (i, 0)),
                   pl.BlockSpec((tm, LANES), lambda i: (i, 0))),
        compiler_params=_params(("parallel",)),
        name="proj_idx",
    )(h, w, cs, e, c)


def _sb_kernel(q_ref, k_ref, v_ref, o_ref, acc_ref, carry_ref, *, t, nh, scale):
    qi = pl.program_id(2)
    tri = (lax.broadcasted_iota(jnp.int32, (t, t), 0)
           > lax.broadcasted_iota(jnp.int32, (t, t), 1)).astype(BF16)
    t_idx = qi * t + lax.broadcasted_iota(jnp.int32, (t, t), 0)
    col = lax.broadcasted_iota(jnp.int32, (t, t), 1)

    def block(h, off, carry):
        hs = slice(h * HEAD_DIM, (h + 1) * HEAD_DIM)
        z = _dot_nt(q_ref[:, hs], k_ref[pl.ds(off, t), hs]) * scale
        mask = (off + col) < t_idx
        e = jnp.exp(-jnp.abs(z))
        log_beta = jnp.minimum(z, 0.0) - jnp.log(1.0 + e)
        l1 = jnp.where(mask, log_beta - z, 0.0)
        hi = l1.astype(BF16)
        lo = (l1 - hi.astype(F32)).astype(BF16)
        suffix = _dot(hi, tri) + _dot(lo, tri)
        a = jnp.where(mask, jnp.exp(log_beta + suffix + carry), 0.0)
        return _dot(a.astype(BF16), v_ref[pl.ds(off, t), hs]), jnp.sum(l1, axis=1, keepdims=True)

    first = jnp.maximum(qi - 1, 0)
    off0 = pl.multiple_of(first * t, t)
    alive = jnp.int32(0)
    for h in range(nh):
        c_late, r_late = block(h, off0 + t, 0.0)
        c_early, r_early = block(h, off0, r_late)
        acc_ref[h] = c_late + c_early
        carry = r_late + r_early
        carry_ref[h] = carry
        alive = jnp.maximum(alive, (jnp.max(carry) > EXP_UNDERFLOW).astype(jnp.int32))

    def body(state):
        j, _ = state
        off = pl.multiple_of(j * t, t)
        alive = jnp.int32(0)
        for h in range(nh):
            carry = carry_ref[h]
            c, r = block(h, off, carry)
            acc_ref[h] += c
            carry = carry + r
            carry_ref[h] = carry
            alive = jnp.maximum(alive, (jnp.max(carry) > EXP_UNDERFLOW).astype(jnp.int32))
        return j - 1, alive

    lax.while_loop(lambda s: (s[0] >= 0) & (s[1] > 0), body, (first - 1, alive))
    for h in range(nh):
        o_ref[:, h * HEAD_DIM:(h + 1) * HEAD_DIM] = acc_ref[h].astype(o_ref.dtype)


def _sb_attention(pa, B, S, t=256, nh=2):
    G = N_SB_HEADS // nh
    w = nh * HEAD_DIM
    return pl.pallas_call(
        functools.partial(_sb_kernel, t=t, nh=nh, scale=HEAD_DIM ** -0.5),
        out_shape=jax.ShapeDtypeStruct((B, S, SB_W), BF16),
        grid=(B, G, S // t),
        in_specs=[pl.BlockSpec((None, t, w), lambda b, g, i: (b, i, g)),
                  pl.BlockSpec((None, S, w), lambda b, g, i: (b, 0, G + g)),
                  pl.BlockSpec((None, S, w), lambda b, g, i: (b, 0, 2 * G + g))],
        out_specs=pl.BlockSpec((None, t, w), lambda b, g, i: (b, i, g)),
        scratch_shapes=[pltpu.VMEM((nh, t, HEAD_DIM), F32), pltpu.VMEM((nh, t, 1), F32)],
        compiler_params=_params(("parallel", "parallel", "arbitrary")),
        name="sb_attention",
    )(pa, pa, pa)


PACKED_ROWS = 16


def _indexer_kernel(iq_ref, iw_ref, ik_ref, bias_ref, rhs_ref, hi_ref, mid_ref, lo_ref, sel_ref, *,
                    tq, tk, to, topk, n_kc, idx_scale):
    qi = pl.program_id(1)
    n_causal = ((qi + 1) * tq + tk - 1) // tk
    G = PACKED_ROWS
    n_groups = tk // G
    r_i = lax.broadcasted_iota(jnp.int32, (LANES, LANES), 0)
    c_i = lax.broadcasted_iota(jnp.int32, (LANES, LANES), 1)
    pick_lo = ((r_i == c_i) & (r_i < IDX_DIM)).astype(BF16)
    pick_hi = ((r_i == c_i) & (r_i >= IDX_DIM)).astype(BF16)
    for p in range(N_IDX_HEADS // 2):
        pair = iq_ref[:, p * LANES:(p + 1) * LANES]
        rhs_ref[2 * p] = _dot_nt(pick_lo, pair).astype(BF16)
        rhs_ref[2 * p + 1] = _dot_nt(pick_hi, pair).astype(BF16)
    w = (iw_ref[...] * idx_scale).T[0:N_IDX_HEADS, :]
    q_idx = qi * tq + lax.broadcasted_iota(jnp.int32, (tk, tq), 1)
    k_row = lax.broadcasted_iota(jnp.int32, (tk, tq), 0)

    def score_chunk(c, _):
        off = pl.multiple_of(c * tk, tk)
        kb = ik_ref[pl.ds(off, tk), :]
        score = jnp.zeros((tk, tq), F32)
        for h in range(N_IDX_HEADS):
            score = score + jnp.maximum(_dot(kb, rhs_ref[h]), 0.0) * w[h:h + 1, :]
        bits = pltpu.bitcast(score, jnp.int32)
        neg = bits < 0
        hi = pltpu.bitcast(bits & jnp.int32(-65536), F32)
        mid = (bits >> 8) & 0xFF
        lo = bits & 0xFF
        mid = jnp.where(neg, 255 - mid, mid)
        lo = jnp.where(neg, 255 - lo, lo)
        hi_ref[c] = jnp.where((off + k_row) <= q_idx, hi, -jnp.inf).astype(BF16)
        mid_ref[c] = mid.astype(F32).astype(BF16)
        lo_ref[c] = lo.astype(F32).astype(BF16)
        return 0

    lax.fori_loop(0, n_causal, score_chunk, 0)

    one = jnp.ones((G, tq), BF16)
    zero = jnp.zeros((G, tq), BF16)

    def tile16(row8):
        return jnp.concatenate([row8, row8], axis=0).astype(BF16)

    def count(v_ref, cand, strict=False):
        def body(c, tot):
            acc = zero
            for g in range(n_groups):
                vs = v_ref[c, g * G:(g + 1) * G, :]
                hit = (vs > cand) if strict else (vs >= cand)
                acc = acc + jnp.where(hit, one, zero)
            return tot + acc.astype(F32)
        tot = lax.fori_loop(0, n_causal, body, jnp.zeros((G, tq), F32))
        return jnp.broadcast_to(jnp.sum(tot, axis=0, keepdims=True), (8, tq))

    def key_to_f32(k):
        b16 = jnp.where(k >= 0, k, k ^ 0x7FFF) & 0xFFFF
        return pltpu.bitcast(b16 << 16, F32)

    def kth_int(v_ref, rank):
        def step(i, ans):
            cand = ans + (jnp.int32(128) >> i).astype(F32)
            return jnp.where(count(v_ref, tile16(cand)) >= rank, cand, ans)
        return tile16(lax.fori_loop(0, 8, step, jnp.zeros((8, tq), F32)))

    def restrict(src_ref, thr, dst_ref):
        def body(c, _):
            for g in range(n_groups):
                gs = slice(g * G, (g + 1) * G)
                dst_ref[c, gs, :] = jnp.where(src_ref[c, gs, :] == thr, dst_ref[c, gs, :], -one)
            return 0
        lax.fori_loop(0, n_causal, body, 0)

    t_q = qi * tq + lax.broadcasted_iota(jnp.int32, (8, tq), 1)
    keff = jnp.minimum(t_q + 1, topk).astype(F32)

    k0 = jnp.zeros((8, tq), jnp.int32)
    ans = jnp.where(count(hi_ref, tile16(key_to_f32(k0))) >= keff, k0, jnp.int32(-32768))

    def step1(i, ans):
        cand = ans | (jnp.int32(1) << (14 - i))
        return jnp.where(count(hi_ref, tile16(key_to_f32(cand))) >= keff, cand, ans)

    t_hi = tile16(key_to_f32(lax.fori_loop(0, 15, step1, ans)))
    r1 = keff - count(hi_ref, t_hi, strict=True)
    restrict(hi_ref, t_hi, mid_ref)
    t_mid = kth_int(mid_ref, r1)
    r2 = r1 - count(mid_ref, t_mid, strict=True)
    restrict(mid_ref, t_mid, lo_ref)
    t_lo = kth_int(lo_ref, r2)

    eye = (lax.broadcasted_iota(jnp.int32, (tq, tq), 0)
           == lax.broadcasted_iota(jnp.int32, (tq, tq), 1)).astype(BF16)

    def write_chunk(c, _):
        for g in range(n_groups):
            gs = slice(g * G, (g + 1) * G)
            sel = ((hi_ref[c, gs, :] > t_hi) | (mid_ref[c, gs, :] > t_mid) | (lo_ref[c, gs, :] >= t_lo))
            sel_ref[gs, :] = jnp.where(sel, one, zero)
        sel_t = _dot_nt(eye, sel_ref[...])
        bias = jnp.where(sel_t > 0.5, 0.0, -jnp.inf).astype(bias_ref.dtype)
        for s in range(tk // to):
            bias_ref[c * (tk // to) + s] = bias[:, s * to:(s + 1) * to]
        return 0

    lax.fori_loop(0, n_causal, write_chunk, 0)

    def fill_chunk(c, _):
        bias_ref[c] = jnp.full((tq, to), -jnp.inf, bias_ref.dtype)
        return 0

    lax.fori_loop(n_causal * (tk // to), n_kc * (tk // to), fill_chunk, 0)


def _indexer(pa, iw, ik2, B, S, topk, tq=256, tk=1024, to=512):
    n_kc = S // tk
    assert tk // PACKED_ROWS <= 256
    field = pltpu.VMEM((n_kc, tk, tq), BF16)
    return pl.pallas_call(
        functools.partial(_indexer_kernel, tq=tq, tk=tk, to=to, topk=topk, n_kc=n_kc,
                          idx_scale=IDX_DIM ** -0.5 * N_IDX_HEADS ** -0.5),
        out_shape=jax.ShapeDtypeStruct((B, S // to, S, to), BF16),
        grid=(B, S // tq),
        in_specs=[pl.BlockSpec((None, tq, IDX_QW), lambda b, i: (b, i, 6)),
                  pl.BlockSpec((None, tq, LANES), lambda b, i: (b, i, 0)),
                  pl.BlockSpec((None, S, LANES), lambda b, i: (b, 0, 0))],
        out_specs=pl.BlockSpec((None, S // to, tq, to), lambda b, i: (b, 0, i, 0)),
        scratch_shapes=[pltpu.VMEM((N_IDX_HEADS, LANES, tq), BF16), field, field, field,
                        pltpu.VMEM((tk, tq), BF16)],
        compiler_params=_params(("parallel", "arbitrary")),
        name="indexer",
    )(pa, iw, ik2)


def _dsa_kernel(q_ref, k_ref, v_ref, m_ref, o_ref, acc_ref, mx_ref, vaug_ref, bias_ref, *, t, scale):
    qi = pl.program_id(1)
    kj = pl.program_id(2)
    reps = t // LANES

    @pl.when(kj == 0)
    def _():
        acc_ref[...] = jnp.zeros_like(acc_ref)
        mx_ref[...] = jnp.full_like(mx_ref, NEG_BIG)
        vaug_ref[:, :, HEAD_DIM:] = jnp.ones((N_DSA_HEADS, t, LANES), BF16)

    @pl.when(kj <= qi)
    def _():
        bias_ref[...] = m_ref[...].astype(F32)
        for h in range(N_DSA_HEADS):
            hs = slice(h * HEAD_DIM, (h + 1) * HEAD_DIM)
            vaug_ref[h, :, 0:HEAD_DIM] = v_ref[:, hs]
            s = _dot_nt(q_ref[:, hs], k_ref[:, hs]) * (scale * LOG2_E) + bias_ref[...]
            m_old = mx_ref[h]
            m_new = jnp.maximum(m_old, jnp.max(s, axis=1, keepdims=True))
            alpha = jnp.exp2(m_old - m_new)
            p = jnp.exp2(s - jnp.tile(m_new, (1, reps)))
            acc_ref[h] = jnp.tile(alpha, (1, 2)) * acc_ref[h] + _dot(p.astype(BF16), vaug_ref[h])
            mx_ref[h] = m_new

    @pl.when(kj == pl.num_programs(2) - 1)
    def _():
        for h in range(N_DSA_HEADS):
            o_ref[:, h * HEAD_DIM:(h + 1) * HEAD_DIM] = (
                acc_ref[h, :, 0:HEAD_DIM] / acc_ref[h, :, HEAD_DIM:]).astype(o_ref.dtype)


def _dsa_attention(pa, bias, B, S, t=512):
    n = S // t
    return pl.pallas_call(
        functools.partial(_dsa_kernel, t=t, scale=HEAD_DIM ** -0.5),
        out_shape=jax.ShapeDtypeStruct((B, S, DSA_W), BF16),
        grid=(B, n, n),
        in_specs=[pl.BlockSpec((None, t, DSA_W), lambda b, i, j: (b, i, 3)),
                  pl.BlockSpec((None, t, DSA_W), lambda b, i, j: (b, jnp.minimum(j, i), 4)),
                  pl.BlockSpec((None, t, DSA_W), lambda b, i, j: (b, jnp.minimum(j, i), 5)),
                  pl.BlockSpec((None, None, t, t), lambda b, i, j: (b, jnp.minimum(j, i), i, 0))],
        out_specs=pl.BlockSpec((None, t, DSA_W), lambda b, i, j: (b, i, 0)),
        scratch_shapes=[pltpu.VMEM((N_DSA_HEADS, t, 2 * HEAD_DIM), F32),
                        pltpu.VMEM((N_DSA_HEADS, t, LANES), F32),
                        pltpu.VMEM((N_DSA_HEADS, t, 2 * HEAD_DIM), BF16),
                        pltpu.VMEM((t, t), F32)],
        compiler_params=_params(("parallel", "parallel", "arbitrary")),
        name="dsa_attention",
    )(pa, pa, pa, bias)


def _merge_kernel(osb_ref, ods_ref, wsb_ref, wds_ref, gsb_ref, gds_ref, o_ref):
    a = _dot(osb_ref[...], wsb_ref[...])
    b = _dot(ods_ref[...], wds_ref[...])
    o_ref[...] = (jax.nn.sigmoid(gsb_ref[...]) * a + jax.nn.sigmoid(gds_ref[...]) * b).astype(o_ref.dtype)


def _merge(o_sb, o_ds, w_branch, gates, tm=1024, tn=1024):
    T = o_sb.shape[0]
    nj = D_MODEL // tn
    return pl.pallas_call(
        _merge_kernel,
        out_shape=jax.ShapeDtypeStruct((T, D_MODEL), BF16),
        grid=(T // tm, nj),
        in_specs=[pl.BlockSpec((tm, SB_W), lambda i, j: (i, 0)),
                  pl.BlockSpec((tm, DSA_W), lambda i, j: (i, 0)),
                  pl.BlockSpec((SB_W, tn), lambda i, j: (0, j)),
                  pl.BlockSpec((DSA_W, tn), lambda i, j: (SB_W // DSA_W, j)),
                  pl.BlockSpec((tm, tn), lambda i, j: (i, j)),
                  pl.BlockSpec((tm, tn), lambda i, j: (i, nj + j))],
        out_specs=pl.BlockSpec((tm, tn), lambda i, j: (i, j)),
        compiler_params=_params(("parallel", "arbitrary")),
        name="merge",
    )(o_sb, o_ds, w_branch, w_branch, gates, gates)


def _matmul_res_kernel(a_ref, w_ref, r_ref, o_ref):
    o_ref[...] = r_ref[...] + _dot(a_ref[...], w_ref[...])


def _matmul_res(a, w, res, tm=1024, tn=1024, name="matmul_res"):
    T, K = a.shape
    N = w.shape[1]
    return pl.pallas_call(
        _matmul_res_kernel,
        out_shape=jax.ShapeDtypeStruct((T, N), F32),
        grid=(T // tm, N // tn),
        in_specs=[pl.BlockSpec((tm, K), lambda i, j: (i, 0)),
                  pl.BlockSpec((K, tn), lambda i, j: (0, j)),
                  pl.BlockSpec((tm, tn), lambda i, j: (i, j))],
        out_specs=pl.BlockSpec((tm, tn), lambda i, j: (i, j)),
        compiler_params=_params(("parallel", "arbitrary")),
        name=name,
    )(a, w, res)


HALO = PACKED_ROWS


def _ffn_kernel(x_ref, xp_ref, g_ref, wg_ref, wv_ref, cwg_ref, cwv_ref, cbg_ref, cbv_ref, wd_ref, o_ref,
                h_ref, *, tm, rows_per_seq):
    i = pl.program_id(0)
    j = pl.program_id(1)

    def norm(x):
        ms = jnp.mean(x * x, axis=-1, keepdims=True)
        return (x * lax.rsqrt(ms + EPS) * g_ref[...]).astype(BF16)

    @pl.when(j == 0)
    def _():
        x = x_ref[...]
        h_ref[HALO:, :] = norm(x)
        first = (i % (rows_per_seq // tm)) == 0
        hp = norm(xp_ref[...])
        h_ref[0:HALO, :] = jnp.where(first, jnp.zeros_like(hp), hp)
        o_ref[...] = x

    h = h_ref[...]

    def conv(w_ref, cw_ref, cb_ref):
        u = _dot(h, w_ref[...])
        u0 = u[HALO:, :]
        u1 = pltpu.roll(u, 1, 0)[HALO:, :]
        u2 = pltpu.roll(u, 2, 0)[HALO:, :]
        return cw_ref[0:1, :] * u2 + cw_ref[1:2, :] * u1 + cw_ref[2:3, :] * u0 + cb_ref[...]

    gate = conv(wg_ref, cwg_ref, cbg_ref)
    val = conv(wv_ref, cwv_ref, cbv_ref)
    act = (gate * jax.nn.sigmoid(gate) * val).astype(BF16)
    o_ref[...] += _dot(act, wd_ref[...])


def _ffn(x1, g, w_up, conv_w, conv_b, w_down, rows_per_seq, tm=512, tf=512):
    T, D = x1.shape
    nj = D_FF // tf
    cb = conv_b.reshape(1, 2 * D_FF)
    return pl.pallas_call(
        functools.partial(_ffn_kernel, tm=tm, rows_per_seq=rows_per_seq),
        out_shape=jax.ShapeDtypeStruct((T, D), F32),
        grid=(T // tm, nj),
        in_specs=[pl.BlockSpec((tm, D), lambda i, j: (i, 0)),
                  pl.BlockSpec((HALO, D), lambda i, j: (jnp.maximum(i * (tm // HALO) - 1, 0), 0)),
                  pl.BlockSpec((1, D), lambda i, j: (0, 0)),
                  pl.BlockSpec((D, tf), lambda i, j: (0, j)),
                  pl.BlockSpec((D, tf), lambda i, j: (0, nj + j)),
                  pl.BlockSpec((CONV_WIDTH, tf), lambda i, j: (0, j)),
                  pl.BlockSpec((CONV_WIDTH, tf), lambda i, j: (0, nj + j)),
                  pl.BlockSpec((1, tf), lambda i, j: (0, j)),
                  pl.BlockSpec((1, tf), lambda i, j: (0, nj + j)),
                  pl.BlockSpec((tf, D), lambda i, j: (j, 0))],
        out_specs=pl.BlockSpec((tm, D), lambda i, j: (i, 0)),
        scratch_shapes=[pltpu.VMEM((HALO + tm, D), BF16)],
        compiler_params=_params(("parallel", "arbitrary")),
        name="ffn",
    )(x1, x1, g.reshape(1, D), w_up, w_up, conv_w, conv_w, cb, cb, w_down)


def _rope_expansion():
    e = np.zeros((ROT_SRC, 6 * LANES), np.float32)
    c = np.zeros((1, 6 * LANES), np.float32)
    src = 0
    for t0, (rot, width) in ((0, (ROPE_DIM, HEAD_DIM)), (3, (IDX_ROPE_DIM, IDX_DIM))):
        half = rot // 2
        cos0, sin0 = src, src + half
        src += rot
        for g in range(LANES // width):
            base = g * width
            for i in range(half):
                e[cos0 + i, t0 * LANES + base + i] = 1.0
                e[cos0 + i, t0 * LANES + base + half + i] = 1.0
                e[sin0 + i, (t0 + 1) * LANES + base + half + i] = 1.0
                e[sin0 + i, (t0 + 2) * LANES + base + i] = -1.0
            c[0, t0 * LANES + base + rot:t0 * LANES + base + width] = 1.0
    return jnp.asarray(e, BF16), jnp.asarray(c, F32)


def _rope_compact(positions):
    pos = positions.reshape(-1).astype(F32)[:, None]
    cols = []
    for rot in (ROPE_DIM, IDX_ROPE_DIM):
        freqs = ROPE_THETA ** (-jnp.arange(0, rot, 2, dtype=F32) / rot)
        ang = pos * freqs
        cols += [jnp.cos(ang), jnp.sin(ang)]
    used = ROPE_DIM + IDX_ROPE_DIM
    cs = jnp.concatenate(cols + [jnp.zeros((pos.shape[0], ROT_SRC - used), F32)], axis=1)

    def top16(v):
        return lax.bitcast_convert_type(lax.bitcast_convert_type(v, jnp.int32) & jnp.int32(-65536), F32)

    hi = top16(cs)
    r = cs - hi
    mid = top16(r)
    lo = r - mid
    return jnp.concatenate([hi, mid, lo], axis=1).astype(BF16)


def kernel(x, positions, attn_norm_g, w_in, q_norm_g, k_norm_g, w_branch, w_out, ffn_norm_g, w_up,
           conv_w, conv_b, w_down):
    B, S, D = x.shape
    assert D == D_MODEL and S % 1024 == 0
    T = B * S
    topk = min(MAX_TOPK, S // 4)
    xf = x.reshape(T, D)

    o_qix = 3 * SB_W + 3 * DSA_W
    o_kix = o_qix + IDX_QW
    o_wix = o_kix + IDX_DIM
    o_g = o_wix + N_IDX_HEADS
    w_in_bf = w_in.astype(BF16)
    w_kix = w_in_bf[:, o_kix:o_wix]
    w_idx = jnp.concatenate([w_kix, w_kix, w_in_bf[:, o_wix:o_g],
                             jnp.zeros((D, LANES - N_IDX_HEADS), BF16)], axis=1)
    w_gate = w_in_bf[:, o_g:]

    cs = _rope_compact(positions)
    rope_e, rope_c = _rope_expansion()

    h1 = _rmsnorm(xf, attn_norm_g)
    pa = _proj_attn(h1, w_in_bf, o_kix, cs, rope_e, rope_c, q_norm_g, k_norm_g)
    gates = _matmul(h1, w_gate, F32, name="proj_gates")
    ik2, iw = _proj_idx(h1, w_idx, cs, rope_e, rope_c)

    pa3 = pa.reshape(B, S, pa.shape[1])
    o_sb = _sb_attention(pa3, B, S)
    bias = _indexer(pa3, iw.reshape(B, S, LANES), ik2.reshape(B, S, LANES), B, S, topk)
    o_ds = _dsa_attention(pa3, bias, B, S)

    merged = _merge(o_sb.reshape(T, SB_W), o_ds.reshape(T, DSA_W), w_branch.astype(BF16), gates)
    x1 = _matmul_res(merged, w_out.astype(BF16), xf, name="out_proj")

    out = _ffn(x1, ffn_norm_g, w_up.astype(BF16), conv_w, conv_b, w_down.astype(BF16), S)
    return out.reshape(B, S, D)
```

```python
import functools

import jax
import jax.numpy as jnp
import numpy as np
from jax import lax
from jax.experimental import pallas as pl
from jax.experimental.pallas import tpu as pltpu

D_MODEL = 2048
HEAD_DIM = 128
N_SB_HEADS = 8
N_DSA_HEADS = 8
N_IDX_HEADS = 16
IDX_DIM = 64
ROPE_THETA = 500000.0
ROPE_DIM = HEAD_DIM // 4
IDX_ROPE_DIM = IDX_DIM // 4
MAX_TOPK = 256
D_FF = 5632
CONV_WIDTH = 3
EPS = 1e-6

SB_W = N_SB_HEADS * HEAD_DIM
DSA_W = N_DSA_HEADS * HEAD_DIM
IDX_QW = N_IDX_HEADS * IDX_DIM

LANES = 128
VMEM_LIMIT = 56 * 1024 * 1024
NEG_BIG = -1e30
EXP_UNDERFLOW = -105.0
LOG2_E = 1.4426950408889634

F32 = jnp.float32
BF16 = jnp.bfloat16


def _params(semantics):
    return pltpu.CompilerParams(dimension_semantics=semantics, vmem_limit_bytes=VMEM_LIMIT)


def _dot(a, b):
    return jnp.dot(a, b, preferred_element_type=F32)


def _dot_nt(a, b):
    return lax.dot_general(a, b, (((1,), (1,)), ((), ())), preferred_element_type=F32)


def _rmsnorm_kernel(x_ref, g_ref, o_ref):
    x = x_ref[...]
    ms = jnp.mean(x * x, axis=-1, keepdims=True)
    o_ref[...] = (x * lax.rsqrt(ms + EPS) * g_ref[...]).astype(o_ref.dtype)


def _rmsnorm(x, g, tm=512):
    T, D = x.shape
    return pl.pallas_call(
        _rmsnorm_kernel,
        out_shape=jax.ShapeDtypeStruct((T, D), BF16),
        grid=(T // tm,),
        in_specs=[pl.BlockSpec((tm, D), lambda i: (i, 0)),
                  pl.BlockSpec((1, D), lambda i: (0, 0))],
        out_specs=pl.BlockSpec((tm, D), lambda i: (i, 0)),
        compiler_params=_params(("parallel",)),
        name="rmsnorm",
    )(x, g.reshape(1, D))


def _rope_head(xh, cos_t, sin_up, sin_dn, half):
    return (xh * cos_t + pltpu.roll(xh, half, 1) * sin_up
            + pltpu.roll(xh, LANES - half, 1) * sin_dn)


ROT_SRC = LANES


def _expand_tables(cs_ref, e_ref, c_ref, lo, hi):
    e = e_ref[:, lo * LANES:hi * LANES]
    out = c_ref[:, lo * LANES:hi * LANES]
    for p in range(3):
        out = out + _dot(cs_ref[:, p * ROT_SRC:(p + 1) * ROT_SRC], e)
    return out


def _proj_attn_kernel(a_ref, w_ref, cs_ref, e_ref, c_ref, qg_ref, kg_ref, o_ref, *, tn):
    acc = _dot(a_ref[...], w_ref[...])
    region = pl.program_id(1) // (SB_W // tn)
    heads = tn // LANES

    def norm_rope(g_ref):
        tab = _expand_tables(cs_ref, e_ref, c_ref, 0, 3)
        for h in range(heads):
            xh = acc[:, h * LANES:(h + 1) * LANES]
            ms = jnp.mean(xh * xh, axis=-1, keepdims=True)
            xh = xh * lax.rsqrt(ms + EPS) * g_ref[...]
            out = _rope_head(xh, tab[:, 0:LANES], tab[:, LANES:2 * LANES],
                             tab[:, 2 * LANES:3 * LANES], ROPE_DIM // 2)
            o_ref[:, h * LANES:(h + 1) * LANES] = out.astype(o_ref.dtype)

    @pl.when((region <= 2) | (region == 5))
    def _():
        o_ref[...] = acc.astype(o_ref.dtype)

    @pl.when(region == 3)
    def _():
        norm_rope(qg_ref)

    @pl.when(region == 4)
    def _():
        norm_rope(kg_ref)

    @pl.when(region == 6)
    def _():
        tab = _expand_tables(cs_ref, e_ref, c_ref, 3, 6)
        for h in range(heads):
            xh = acc[:, h * LANES:(h + 1) * LANES]
            out = _rope_head(xh, tab[:, 0:LANES], tab[:, LANES:2 * LANES],
                             tab[:, 2 * LANES:3 * LANES], IDX_ROPE_DIM // 2)
            o_ref[:, h * LANES:(h + 1) * LANES] = out.astype(o_ref.dtype)


def _proj_attn(h, w, N, cs, e, c, qg, kg, tm=1024, tn=1024):
    T, K = h.shape
    return pl.pallas_call(
        functools.partial(_proj_attn_kernel, tn=tn),
        out_shape=jax.ShapeDtypeStruct((T, N), BF16),
        grid=(T // tm, N // tn),
        in_specs=[pl.BlockSpec((tm, K), lambda i, j: (i, 0)),
                  pl.BlockSpec((K, tn), lambda i, j: (0, j)),
                  pl.BlockSpec((tm, 3 * ROT_SRC), lambda i, j: (i, 0)),
                  pl.BlockSpec((ROT_SRC, 6 * LANES), lambda i, j: (0, 0)),
                  pl.BlockSpec((1, 6 * LANES), lambda i, j: (0, 0)),
                  pl.BlockSpec((1, LANES), lambda i, j: (0, 0)),
                  pl.BlockSpec((1, LANES), lambda i, j: (0, 0))],
        out_specs=pl.BlockSpec((tm, tn), lambda i, j: (i, j)),
        compiler_params=_params(("parallel", "arbitrary")),
        name="proj_attn",
    )(h, w, cs, e, c, qg.reshape(1, LANES), kg.reshape(1, LANES))


def _matmul_kernel(a_ref, w_ref, o_ref):
    o_ref[...] = _dot(a_ref[...], w_ref[...]).astype(o_ref.dtype)


def _matmul(a, w, out_dtype, tm=1024, tn=1024, name="matmul"):
    T, K = a.shape
    N = w.shape[1]
    return pl.pallas_call(
        _matmul_kernel,
        out_shape=jax.ShapeDtypeStruct((T, N), out_dtype),
        grid=(T // tm, N // tn),
        in_specs=[pl.BlockSpec((tm, K), lambda i, j: (i, 0)),
                  pl.BlockSpec((K, tn), lambda i, j: (0, j))],
        out_specs=pl.BlockSpec((tm, tn), lambda i, j: (i, j)),
        compiler_params=_params(("parallel", "arbitrary")),
        name=name,
    )(a, w)


def _proj_idx_kernel(a_ref, w_ref, cs_ref, e_ref, c_ref, ik_ref, iw_ref):
    acc = _dot(a_ref[...], w_ref[...])
    kk = acc[:, 0:LANES]
    tab = _expand_tables(cs_ref, e_ref, c_ref, 3, 6)
    out = _rope_head(kk, tab[:, 0:LANES], tab[:, LANES:2 * LANES], tab[:, 2 * LANES:3 * LANES],
                     IDX_ROPE_DIM // 2)
    ik_ref[...] = out.astype(ik_ref.dtype)
    iw_ref[...] = acc[:, LANES:2 * LANES]


def _proj_idx(h, w, cs, e, c, tm=512):
    T, K = h.shape
    return pl.pallas_call(
        _proj_idx_kernel,
        out_shape=(jax.ShapeDtypeStruct((T, LANES), BF16), jax.ShapeDtypeStruct((T, LANES), F32)),
        grid=(T // tm,),
        in_specs=[pl.BlockSpec((tm, K), lambda i: (i, 0)),
                  pl.BlockSpec((K, 2 * LANES), lambda i: (0, 0)),
                  pl.BlockSpec((tm, 3 * ROT_SRC), lambda i: (i, 0)),
                  pl.BlockSpec((ROT_SRC, 6 * LANES), lambda i: (0, 0)),
                  pl.BlockSpec((1, 6 * LANES), lambda i: (0, 0))],
        out_specs=(pl.BlockSpec((tm, LANES), lambda i: (i, 0)),
                   pl.BlockSpec((tm, LANES), lambda i: (i, 0))),
        compiler_params=_params(("parallel",)),
        name="proj_idx",
    )(h, w, cs, e, c)


def _sb_kernel(q_ref, k_ref, v_ref, o_ref, acc_ref, carry_ref, *, t, nh, scale):
    qi = pl.program_id(2)
    tri = (lax.broadcasted_iota(jnp.int32, (t, t), 0)
           > lax.broadcasted_iota(jnp.int32, (t, t), 1)).astype(BF16)
    t_idx = qi * t + lax.broadcasted_iota(jnp.int32, (t, t), 0)
    col = lax.broadcasted_iota(jnp.int32, (t, t), 1)

    def block(h, off, carry):
        hs = slice(h * HEAD_DIM, (h + 1) * HEAD_DIM)
        z = _dot_nt(q_ref[:, hs], k_ref[pl.ds(off, t), hs]) * scale
        mask = (off + col) < t_idx
        e = jnp.exp(-jnp.abs(z))
        log_beta = jnp.minimum(z, 0.0) - jnp.log(1.0 + e)
        l1 = jnp.where(mask, log_beta - z, 0.0)
        hi = l1.astype(BF16)
        lo = (l1 - hi.astype(F32)).astype(BF16)
        suffix = _dot(hi, tri) + _dot(lo, tri)
        a = jnp.where(mask, jnp.exp(log_beta + suffix + carry), 0.0)
        return _dot(a.astype(BF16), v_ref[pl.ds(off, t), hs]), jnp.sum(l1, axis=1, keepdims=True)

    first = jnp.maximum(qi - 1, 0)
    off0 = pl.multiple_of(first * t, t)
    alive = jnp.int32(0)
    for h in range(nh):
        c_late, r_late = block(h, off0 + t, 0.0)
        c_early, r_early = block(h, off0, r_late)
        acc_ref[h] = c_late + c_early
        carry = r_late + r_early
        carry_ref[h] = carry
        alive = jnp.maximum(alive, (jnp.max(carry) > EXP_UNDERFLOW).astype(jnp.int32))

    def body(state):
        j, _ = state
        off = pl.multiple_of(j * t, t)
        alive = jnp.int32(0)
        for h in range(nh):
            carry = carry_ref[h]
            c, r = block(h, off, carry)
            acc_ref[h] += c
            carry = carry + r
            carry_ref[h] = carry
            alive = jnp.maximum(alive, (jnp.max(carry) > EXP_UNDERFLOW).astype(jnp.int32))
        return j - 1, alive

    lax.while_loop(lambda s: (s[0] >= 0) & (s[1] > 0), body, (first - 1, alive))
    for h in range(nh):
        o_ref[:, h * HEAD_DIM:(h + 1) * HEAD_DIM] = acc_ref[h].astype(o_ref.dtype)


def _sb_attention(pa, B, S, t=256, nh=4):
    G = N_SB_HEADS // nh
    w = nh * HEAD_DIM
    return pl.pallas_call(
        functools.partial(_sb_kernel, t=t, nh=nh, scale=HEAD_DIM ** -0.5),
        out_shape=jax.ShapeDtypeStruct((B, S, SB_W), BF16),
        grid=(B, G, S // t),
        in_specs=[pl.BlockSpec((None, t, w), lambda b, g, i: (b, i, g)),
                  pl.BlockSpec((None, S, w), lambda b, g, i: (b, 0, G + g)),
                  pl.BlockSpec((None, S, w), lambda b, g, i: (b, 0, 2 * G + g))],
        out_specs=pl.BlockSpec((None, t, w), lambda b, g, i: (b, i, g)),
        scratch_shapes=[pltpu.VMEM((nh, t, HEAD_DIM), F32), pltpu.VMEM((nh, t, 1), F32)],
        compiler_params=_params(("parallel", "parallel", "arbitrary")),
        name="sb_attention",
    )(pa, pa, pa)


PACKED_ROWS = 16
COUNT_CHAINS = 4


def _indexer_kernel(iq_ref, iw_ref, ik_ref, bias_ref, rhs_ref, hi_ref, mid_ref, lo_ref, sel_ref, *,
                    tq, tk, to, topk, n_kc, idx_scale):
    qi = pl.program_id(1)
    n_causal = ((qi + 1) * tq + tk - 1) // tk
    G = PACKED_ROWS
    n_groups = tk // G
    r_i = lax.broadcasted_iota(jnp.int32, (LANES, LANES), 0)
    c_i = lax.broadcasted_iota(jnp.int32, (LANES, LANES), 1)
    pick_lo = ((r_i == c_i) & (r_i < IDX_DIM)).astype(BF16)
    pick_hi = ((r_i == c_i) & (r_i >= IDX_DIM)).astype(BF16)
    for p in range(N_IDX_HEADS // 2):
        pair = iq_ref[:, p * LANES:(p + 1) * LANES]
        rhs_ref[2 * p] = _dot_nt(pick_lo, pair).astype(BF16)
        rhs_ref[2 * p + 1] = _dot_nt(pick_hi, pair).astype(BF16)
    w = (iw_ref[...] * idx_scale).T[0:N_IDX_HEADS, :]
    q_idx = qi * tq + lax.broadcasted_iota(jnp.int32, (tk, tq), 1)
    k_row = lax.broadcasted_iota(jnp.int32, (tk, tq), 0)

    def score_chunk(c, _):
        off = pl.multiple_of(c * tk, tk)
        kb = ik_ref[pl.ds(off, tk), :]
        score = jnp.zeros((tk, tq), F32)
        for h in range(N_IDX_HEADS):
            score = score + jnp.maximum(_dot(kb, rhs_ref[h]), 0.0) * w[h:h + 1, :]
        bits = pltpu.bitcast(score, jnp.int32)
        neg = bits < 0
        hi = pltpu.bitcast(bits & jnp.int32(-65536), F32)
        mid = (bits >> 8) & 0xFF
        lo = bits & 0xFF
        mid = jnp.where(neg, 255 - mid, mid)
        lo = jnp.where(neg, 255 - lo, lo)
        hi_ref[c] = jnp.where((off + k_row) <= q_idx, hi, -jnp.inf).astype(BF16)
        mid_ref[c] = mid.astype(F32).astype(BF16)
        lo_ref[c] = lo.astype(F32).astype(BF16)
        return 0

    lax.fori_loop(0, n_causal, score_chunk, 0)

    one = jnp.ones((G, tq), BF16)
    zero = jnp.zeros((G, tq), BF16)

    def tile16(row8):
        return jnp.concatenate([row8, row8], axis=0).astype(BF16)

    def count(v_ref, cand, strict=False):
        def body(c, tot):
            accs = [zero] * COUNT_CHAINS
            for g in range(n_groups):
                vs = v_ref[c, g * G:(g + 1) * G, :]
                hit = (vs > cand) if strict else (vs >= cand)
                accs[g % COUNT_CHAINS] = accs[g % COUNT_CHAINS] + jnp.where(hit, one, zero)
            acc = accs[0]
            for a in accs[1:]:
                acc = acc + a
            return tot + acc.astype(F32)
        tot = lax.fori_loop(0, n_causal, body, jnp.zeros((G, tq), F32))
        return jnp.broadcast_to(jnp.sum(tot, axis=0, keepdims=True), (8, tq))

    def key_to_f32(k):
        b16 = jnp.where(k >= 0, k, k ^ 0x7FFF) & 0xFFFF
        return pltpu.bitcast(b16 << 16, F32)

    def kth_int(v_ref, rank):
        def step(i, ans):
            cand = ans + (jnp.int32(128) >> i).astype(F32)
            return jnp.where(count(v_ref, tile16(cand)) >= rank, cand, ans)
        return tile16(lax.fori_loop(0, 8, step, jnp.zeros((8, tq), F32)))

    def restrict(src_ref, thr, dst_ref):
        def body(c, _):
            for g in range(n_groups):
                gs = slice(g * G, (g + 1) * G)
                dst_ref[c, gs, :] = jnp.where(src_ref[c, gs, :] == thr, dst_ref[c, gs, :], -one)
            return 0
        lax.fori_loop(0, n_causal, body, 0)

    t_q = qi * tq + lax.broadcasted_iota(jnp.int32, (8, tq), 1)
    keff = jnp.minimum(t_q + 1, topk).astype(F32)

    k0 = jnp.zeros((8, tq), jnp.int32)
    ans = jnp.where(count(hi_ref, tile16(key_to_f32(k0))) >= keff, k0, jnp.int32(-32768))

    def step1(i, ans):
        cand = ans | (jnp.int32(1) << (14 - i))
        return jnp.where(count(hi_ref, tile16(key_to_f32(cand))) >= keff, cand, ans)

    t_hi = tile16(key_to_f32(lax.fori_loop(0, 15, step1, ans)))
    r1 = keff - count(hi_ref, t_hi, strict=True)
    restrict(hi_ref, t_hi, mid_ref)
    t_mid = kth_int(mid_ref, r1)
    r2 = r1 - count(mid_ref, t_mid, strict=True)
    restrict(mid_ref, t_mid, lo_ref)
    zero8 = jnp.zeros((8, tq), F32)

    def unresolved(cnt):
        return jnp.max(jnp.where(cnt == r2, 0.0, 1.0)).astype(jnp.int32)

    cnt0 = count(lo_ref, tile16(zero8))

    def step3(state):
        i, ans, cnt_ans, _ = state
        cand = ans + (jnp.int32(128) >> i).astype(F32)
        cnt = count(lo_ref, tile16(cand))
        ok = cnt >= r2
        ans = jnp.where(ok, cand, ans)
        cnt_ans = jnp.where(ok, cnt, cnt_ans)
        return i + 1, ans, cnt_ans, unresolved(cnt_ans)

    _, ans3, _, _ = lax.while_loop(lambda s: (s[0] < 8) & (s[3] > 0), step3,
                                   (jnp.int32(0), zero8, cnt0, unresolved(cnt0)))
    t_lo = tile16(ans3)

    eye = (lax.broadcasted_iota(jnp.int32, (tq, tq), 0)
           == lax.broadcasted_iota(jnp.int32, (tq, tq), 1)).astype(BF16)

    def write_chunk(c, _):
        for g in range(n_groups):
            gs = slice(g * G, (g + 1) * G)
            sel = ((hi_ref[c, gs, :] > t_hi) | (mid_ref[c, gs, :] > t_mid) | (lo_ref[c, gs, :] >= t_lo))
            sel_ref[gs, :] = jnp.where(sel, one, zero)
        sel_t = _dot_nt(eye, sel_ref[...])
        bias = jnp.where(sel_t > 0.5, 0.0, -jnp.inf).astype(bias_ref.dtype)
        for s in range(tk // to):
            bias_ref[c * (tk // to) + s] = bias[:, s * to:(s + 1) * to]
        return 0

    lax.fori_loop(0, n_causal, write_chunk, 0)

    def fill_chunk(c, _):
        bias_ref[c] = jnp.full((tq, to), -jnp.inf, bias_ref.dtype)
        return 0

    lax.fori_loop(n_causal * (tk // to), n_kc * (tk // to), fill_chunk, 0)


def _indexer(pa, iw, ik2, B, S, topk, tq=256, tk=1024, to=512):
    n_kc = S // tk
    assert tk // PACKED_ROWS <= 256
    field = pltpu.VMEM((n_kc, tk, tq), BF16)
    return pl.pallas_call(
        functools.partial(_indexer_kernel, tq=tq, tk=tk, to=to, topk=topk, n_kc=n_kc,
                          idx_scale=IDX_DIM ** -0.5 * N_IDX_HEADS ** -0.5),
        out_shape=jax.ShapeDtypeStruct((B, S // to, S, to), BF16),
        grid=(B, S // tq),
        in_specs=[pl.BlockSpec((None, tq, IDX_QW), lambda b, i: (b, i, 6)),
                  pl.BlockSpec((None, tq, LANES), lambda b, i: (b, i, 0)),
                  pl.BlockSpec((None, S, LANES), lambda b, i: (b, 0, 0))],
        out_specs=pl.BlockSpec((None, S // to, tq, to), lambda b, i: (b, 0, i, 0)),
        scratch_shapes=[pltpu.VMEM((N_IDX_HEADS, LANES, tq), BF16), field, field, field,
                        pltpu.VMEM((tk, tq), BF16)],
        compiler_params=_params(("parallel", "arbitrary")),
        name="indexer",
    )(pa, iw, ik2)


def _dsa_kernel(q_ref, k_ref, v_ref, m_ref, o_ref, acc_ref, mx_ref, vaug_ref, bias_ref, *, t, scale):
    qi = pl.program_id(1)
    kj = pl.program_id(2)
    reps = t // LANES

    @pl.when(kj == 0)
    def _():
        acc_ref[...] = jnp.zeros_like(acc_ref)
        mx_ref[...] = jnp.full_like(mx_ref, NEG_BIG)
        vaug_ref[:, :, HEAD_DIM:] = jnp.ones((N_DSA_HEADS, t, LANES), BF16)

    @pl.when(kj <= qi)
    def _():
        bias_ref[...] = m_ref[...].astype(F32)
        for h in range(N_DSA_HEADS):
            hs = slice(h * HEAD_DIM, (h + 1) * HEAD_DIM)
            vaug_ref[h, :, 0:HEAD_DIM] = v_ref[:, hs]
            s = _dot_nt(q_ref[:, hs], k_ref[:, hs]) * (scale * LOG2_E) + bias_ref[...]
            m_old = mx_ref[h]
            m_new = jnp.maximum(m_old, jnp.max(s, axis=1, keepdims=True))
            alpha = jnp.exp2(m_old - m_new)
            p = jnp.exp2(s - jnp.tile(m_new, (1, reps)))
            acc_ref[h] = jnp.tile(alpha, (1, 2)) * acc_ref[h] + _dot(p.astype(BF16), vaug_ref[h])
            mx_ref[h] = m_new

    @pl.when(kj == pl.num_programs(2) - 1)
    def _():
        for h in range(N_DSA_HEADS):
            o_ref[:, h * HEAD_DIM:(h + 1) * HEAD_DIM] = (
                acc_ref[h, :, 0:HEAD_DIM] / acc_ref[h, :, HEAD_DIM:]).astype(o_ref.dtype)


def _dsa_attention(pa, bias, B, S, t=512):
    n = S // t
    return pl.pallas_call(
        functools.partial(_dsa_kernel, t=t, scale=HEAD_DIM ** -0.5),
        out_shape=jax.ShapeDtypeStruct((B, S, DSA_W), BF16),
        grid=(B, n, n),
        in_specs=[pl.BlockSpec((None, t, DSA_W), lambda b, i, j: (b, i, 3)),
                  pl.BlockSpec((None, t, DSA_W), lambda b, i, j: (b, jnp.minimum(j, i), 4)),
                  pl.BlockSpec((None, t, DSA_W), lambda b, i, j: (b, jnp.minimum(j, i), 5)),
                  pl.BlockSpec((None, None, t, t), lambda b, i, j: (b, jnp.minimum(j, i), i, 0))],
        out_specs=pl.BlockSpec((None, t, DSA_W), lambda b, i, j: (b, i, 0)),
        scratch_shapes=[pltpu.VMEM((N_DSA_HEADS, t, 2 * HEAD_DIM), F32),
                        pltpu.VMEM((N_DSA_HEADS, t, LANES), F32),
                        pltpu.VMEM((N_DSA_HEADS, t, 2 * HEAD_DIM), BF16),
                        pltpu.VMEM((t, t), F32)],
        compiler_params=_params(("parallel", "parallel", "arbitrary")),
        name="dsa_attention",
    )(pa, pa, pa, bias)


def _merge_kernel(osb_ref, ods_ref, wsb_ref, wds_ref, gsb_ref, gds_ref, o_ref):
    a = _dot(osb_ref[...], wsb_ref[...])
    b = _dot(ods_ref[...], wds_ref[...])
    o_ref[...] = (jax.nn.sigmoid(gsb_ref[...]) * a + jax.nn.sigmoid(gds_ref[...]) * b).astype(o_ref.dtype)


def _merge(o_sb, o_ds, w_branch, gates, tm=1024, tn=1024):
    T = o_sb.shape[0]
    nj = D_MODEL // tn
    return pl.pallas_call(
        _merge_kernel,
        out_shape=jax.ShapeDtypeStruct((T, D_MODEL), BF16),
        grid=(T // tm, nj),
        in_specs=[pl.BlockSpec((tm, SB_W), lambda i, j: (i, 0)),
                  pl.BlockSpec((tm, DSA_W), lambda i, j: (i, 0)),
                  pl.BlockSpec((SB_W, tn), lambda i, j: (0, j)),
                  pl.BlockSpec((DSA_W, tn), lambda i, j: (SB_W // DSA_W, j)),
                  pl.BlockSpec((tm, tn), lambda i, j: (i, j)),
                  pl.BlockSpec((tm, tn), lambda i, j: (i, nj + j))],
        out_specs=pl.BlockSpec((tm, tn), lambda i, j: (i, j)),
        compiler_params=_params(("parallel", "arbitrary")),
        name="merge",
    )(o_sb, o_ds, w_branch, w_branch, gates, gates)


def _matmul_res_kernel(a_ref, w_ref, r_ref, o_ref):
    o_ref[...] = r_ref[...] + _dot(a_ref[...], w_ref[...])


def _matmul_res(a, w, res, tm=1024, tn=1024, name="matmul_res"):
    T, K = a.shape
    N = w.shape[1]
    return pl.pallas_call(
        _matmul_res_kernel,
        out_shape=jax.ShapeDtypeStruct((T, N), F32),
        grid=(T // tm, N // tn),
        in_specs=[pl.BlockSpec((tm, K), lambda i, j: (i, 0)),
                  pl.BlockSpec((K, tn), lambda i, j: (0, j)),
                  pl.BlockSpec((tm, tn), lambda i, j: (i, j))],
        out_specs=pl.BlockSpec((tm, tn), lambda i, j: (i, j)),
        compiler_params=_params(("parallel", "arbitrary")),
        name=name,
    )(a, w, res)


HALO = PACKED_ROWS


def _ffn_kernel(x_ref, xp_ref, g_ref, wg_ref, wv_ref, cwg_ref, cwv_ref, cbg_ref, cbv_ref, wd_ref, o_ref,
                h_ref, *, tm, rows_per_seq):
    i = pl.program_id(0)
    j = pl.program_id(1)

    def norm(x):
        ms = jnp.mean(x * x, axis=-1, keepdims=True)
        return (x * lax.rsqrt(ms + EPS) * g_ref[...]).astype(BF16)

    @pl.when(j == 0)
    def _():
        x = x_ref[...]
        h_ref[HALO:, :] = norm(x)
        first = (i % (rows_per_seq // tm)) == 0
        hp = norm(xp_ref[...])
        h_ref[0:HALO, :] = jnp.where(first, jnp.zeros_like(hp), hp)
        o_ref[...] = x

    h = h_ref[...]

    def conv(w_ref, cw_ref, cb_ref):
        u = _dot(h, w_ref[...])
        u0 = u[HALO:, :]
        u1 = pltpu.roll(u, 1, 0)[HALO:, :]
        u2 = pltpu.roll(u, 2, 0)[HALO:, :]
        return cw_ref[0:1, :] * u2 + cw_ref[1:2, :] * u1 + cw_ref[2:3, :] * u0 + cb_ref[...]

    gate = conv(wg_ref, cwg_ref, cbg_ref)
    val = conv(wv_ref, cwv_ref, cbv_ref)
    act = (gate * jax.nn.sigmoid(gate) * val).astype(BF16)
    o_ref[...] += _dot(act, wd_ref[...])


def _ffn(x1, g, w_up, conv_w, conv_b, w_down, rows_per_seq, tm=512, tf=512):
    T, D = x1.shape
    nj = D_FF // tf
    cb = conv_b.reshape(1, 2 * D_FF)
    return pl.pallas_call(
        functools.partial(_ffn_kernel, tm=tm, rows_per_seq=rows_per_seq),
        out_shape=jax.ShapeDtypeStruct((T, D), F32),
        grid=(T // tm, nj),
        in_specs=[pl.BlockSpec((tm, D), lambda i, j: (i, 0)),
                  pl.BlockSpec((HALO, D), lambda i, j: (jnp.maximum(i * (tm // HALO) - 1, 0), 0)),
                  pl.BlockSpec((1, D), lambda i, j: (0, 0)),
                  pl.BlockSpec((D, tf), lambda i, j: (0, j)),
                  pl.BlockSpec((D, tf), lambda i, j: (0, nj + j)),
                  pl.BlockSpec((CONV_WIDTH, tf), lambda i, j: (0, j)),
                  pl.BlockSpec((CONV_WIDTH, tf), lambda i, j: (0, nj + j)),
                  pl.BlockSpec((1, tf), lambda i, j: (0, j)),
                  pl.BlockSpec((1, tf), lambda i, j: (0, nj + j)),
                  pl.BlockSpec((tf, D), lambda i, j: (j, 0))],
        out_specs=pl.BlockSpec((tm, D), lambda i, j: (i, 0)),
        scratch_shapes=[pltpu.VMEM((HALO + tm, D), BF16)],
        compiler_params=_params(("parallel", "arbitrary")),
        name="ffn",
    )(x1, x1, g.reshape(1, D), w_up, w_up, conv_w, conv_w, cb, cb, w_down)


def _rope_expansion():
    e = np.zeros((ROT_SRC, 6 * LANES), np.float32)
    c = np.zeros((1, 6 * LANES), np.float32)
    src = 0
    for t0, (rot, width) in ((0, (ROPE_DIM, HEAD_DIM)), (3, (IDX_ROPE_DIM, IDX_DIM))):
        half = rot // 2
        cos0, sin0 = src, src + half
        src += rot
        for g in range(LANES // width):
            base = g * width
            for i in range(half):
                e[cos0 + i, t0 * LANES + base + i] = 1.0
                e[cos0 + i, t0 * LANES + base + half + i] = 1.0
                e[sin0 + i, (t0 + 1) * LANES + base + half + i] = 1.0
                e[sin0 + i, (t0 + 2) * LANES + base + i] = -1.0
            c[0, t0 * LANES + base + rot:t0 * LANES + base + width] = 1.0
    return jnp.asarray(e, BF16), jnp.asarray(c, F32)


def _rope_compact(positions):
    pos = positions.reshape(-1).astype(F32)[:, None]
    cols = []
    for rot in (ROPE_DIM, IDX_ROPE_DIM):
        freqs = ROPE_THETA ** (-jnp.arange(0, rot, 2, dtype=F32) / rot)
        ang = pos * freqs
        cols += [jnp.cos(ang), jnp.sin(ang)]
    used = ROPE_DIM + IDX_ROPE_DIM
    cs = jnp.concatenate(cols + [jnp.zeros((pos.shape[0], ROT_SRC - used), F32)], axis=1)

    def top16(v):
        return lax.bitcast_convert_type(lax.bitcast_convert_type(v, jnp.int32) & jnp.int32(-65536), F32)

    hi = top16(cs)
    r = cs - hi
    mid = top16(r)
    lo = r - mid
    return jnp.concatenate([hi, mid, lo], axis=1).astype(BF16)


def kernel(x, positions, attn_norm_g, w_in, q_norm_g, k_norm_g, w_branch, w_out, ffn_norm_g, w_up,
           conv_w, conv_b, w_down):
    B, S, D = x.shape
    assert D == D_MODEL and S % 1024 == 0
    T = B * S
    topk = min(MAX_TOPK, S // 4)
    xf = x.reshape(T, D)

    o_qix = 3 * SB_W + 3 * DSA_W
    o_kix = o_qix + IDX_QW
    o_wix = o_kix + IDX_DIM
    o_g = o_wix + N_IDX_HEADS
    w_in_bf = w_in.astype(BF16)
    w_kix = w_in_bf[:, o_kix:o_wix]
    w_idx = jnp.concatenate([w_kix, w_kix, w_in_bf[:, o_wix:o_g],
                             jnp.zeros((D, LANES - N_IDX_HEADS), BF16)], axis=1)
    w_gate = w_in_bf[:, o_g:]

    cs = _rope_compact(positions)
    rope_e, rope_c = _rope_expansion()

    h1 = _rmsnorm(xf, attn_norm_g)
    pa = _proj_attn(h1, w_in_bf, o_kix, cs, rope_e, rope_c, q_norm_g, k_norm_g)
    gates = _matmul(h1, w_gate, F32, name="proj_gates")
    ik2, iw = _proj_idx(h1, w_idx, cs, rope_e, rope_c)

    pa3 = pa.reshape(B, S, pa.shape[1])
    o_sb = _sb_attention(pa3, B, S)
    bias = _indexer(pa3, iw.reshape(B, S, LANES), ik2.reshape(B, S, LANES), B, S, topk)
    o_ds = _dsa_attention(pa3, bias, B, S)

    merged = _merge(o_sb.reshape(T, SB_W), o_ds.reshape(T, DSA_W), w_branch.astype(BF16), gates)
    x1 = _matmul_res(merged, w_out.astype(BF16), xf, name="out_proj")

    out = _ffn(x1, ffn_norm_g, w_up.astype(BF16), conv_w, conv_b, w_down.astype(BF16), S)
    return out.reshape(B, S, D)
```

```python
import functools

import jax
import jax.numpy as jnp
import numpy as np
from jax import lax
from jax.experimental import pallas as pl
from jax.experimental.pallas import tpu as pltpu

D_MODEL = 2048
HEAD_DIM = 128
N_SB_HEADS = 8
N_DSA_HEADS = 8
N_IDX_HEADS = 16
IDX_DIM = 64
ROPE_THETA = 500000.0
ROPE_DIM = HEAD_DIM // 4
IDX_ROPE_DIM = IDX_DIM // 4
MAX_TOPK = 256
D_FF = 5632
CONV_WIDTH = 3
EPS = 1e-6

SB_W = N_SB_HEADS * HEAD_DIM
DSA_W = N_DSA_HEADS * HEAD_DIM
IDX_QW = N_IDX_HEADS * IDX_DIM

LANES = 128
VMEM_LIMIT = 56 * 1024 * 1024
NEG_BIG = -1e30
EXP_UNDERFLOW = -105.0
LOG2_E = 1.4426950408889634

F32 = jnp.float32
BF16 = jnp.bfloat16


def _params(semantics):
    return pltpu.CompilerParams(dimension_semantics=semantics, vmem_limit_bytes=VMEM_LIMIT)


def _dot(a, b):
    return jnp.dot(a, b, preferred_element_type=F32)


def _dot_nt(a, b):
    return lax.dot_general(a, b, (((1,), (1,)), ((), ())), preferred_element_type=F32)


def _rmsnorm_kernel(x_ref, g_ref, o_ref):
    x = x_ref[...]
    ms = jnp.mean(x * x, axis=-1, keepdims=True)
    o_ref[...] = (x * lax.rsqrt(ms + EPS) * g_ref[...]).astype(o_ref.dtype)


def _rmsnorm(x, g, tm=512):
    T, D = x.shape
    return pl.pallas_call(
        _rmsnorm_kernel,
        out_shape=jax.ShapeDtypeStruct((T, D), BF16),
        grid=(T // tm,),
        in_specs=[pl.BlockSpec((tm, D), lambda i: (i, 0)),
                  pl.BlockSpec((1, D), lambda i: (0, 0))],
        out_specs=pl.BlockSpec((tm, D), lambda i: (i, 0)),
        compiler_params=_params(("parallel",)),
        name="rmsnorm",
    )(x, g.reshape(1, D))


def _rope_head(xh, cos_t, sin_up, sin_dn, half):
    return (xh * cos_t + pltpu.roll(xh, half, 1) * sin_up
            + pltpu.roll(xh, LANES - half, 1) * sin_dn)


ROT_SRC = LANES


def _expand_tables(cs_ref, e_ref, c_ref, lo, hi):
    e = e_ref[:, lo * LANES:hi * LANES]
    out = c_ref[:, lo * LANES:hi * LANES]
    for p in range(3):
        out = out + _dot(cs_ref[:, p * ROT_SRC:(p + 1) * ROT_SRC], e)
    return out


def _proj_attn_kernel(a_ref, w_ref, cs_ref, e_ref, c_ref, qg_ref, kg_ref, o_ref, *, tn):
    acc = _dot(a_ref[...], w_ref[...])
    region = pl.program_id(1) // (SB_W // tn)
    heads = tn // LANES

    def norm_rope(g_ref):
        tab = _expand_tables(cs_ref, e_ref, c_ref, 0, 3)
        for h in range(heads):
            xh = acc[:, h * LANES:(h + 1) * LANES]
            ms = jnp.mean(xh * xh, axis=-1, keepdims=True)
            xh = xh * lax.rsqrt(ms + EPS) * g_ref[...]
            out = _rope_head(xh, tab[:, 0:LANES], tab[:, LANES:2 * LANES],
                             tab[:, 2 * LANES:3 * LANES], ROPE_DIM // 2)
            o_ref[:, h * LANES:(h + 1) * LANES] = out.astype(o_ref.dtype)

    @pl.when((region <= 2) | (region == 5))
    def _():
        o_ref[...] = acc.astype(o_ref.dtype)

    @pl.when(region == 3)
    def _():
        norm_rope(qg_ref)

    @pl.when(region == 4)
    def _():
        norm_rope(kg_ref)

    @pl.when(region == 6)
    def _():
        tab = _expand_tables(cs_ref, e_ref, c_ref, 3, 6)
        for h in range(heads):
            xh = acc[:, h * LANES:(h + 1) * LANES]
            out = _rope_head(xh, tab[:, 0:LANES], tab[:, LANES:2 * LANES],
                             tab[:, 2 * LANES:3 * LANES], IDX_ROPE_DIM // 2)
            o_ref[:, h * LANES:(h + 1) * LANES] = out.astype(o_ref.dtype)


def _proj_attn(h, w, N, cs, e, c, qg, kg, tm=1024, tn=1024):
    T, K = h.shape
    return pl.pallas_call(
        functools.partial(_proj_attn_kernel, tn=tn),
        out_shape=jax.ShapeDtypeStruct((T, N), BF16),
        grid=(T // tm, N // tn),
        in_specs=[pl.BlockSpec((tm, K), lambda i, j: (i, 0)),
                  pl.BlockSpec((K, tn), lambda i, j: (0, j)),
                  pl.BlockSpec((tm, 3 * ROT_SRC), lambda i, j: (i, 0)),
                  pl.BlockSpec((ROT_SRC, 6 * LANES), lambda i, j: (0, 0)),
                  pl.BlockSpec((1, 6 * LANES), lambda i, j: (0, 0)),
                  pl.BlockSpec((1, LANES), lambda i, j: (0, 0)),
                  pl.BlockSpec((1, LANES), lambda i, j: (0, 0))],
        out_specs=pl.BlockSpec((tm, tn), lambda i, j: (i, j)),
        compiler_params=_params(("parallel", "arbitrary")),
        name="proj_attn",
    )(h, w, cs, e, c, qg.reshape(1, LANES), kg.reshape(1, LANES))


def _matmul_kernel(a_ref, w_ref, o_ref):
    o_ref[...] = _dot(a_ref[...], w_ref[...]).astype(o_ref.dtype)


def _matmul(a, w, out_dtype, tm=1024, tn=1024, name="matmul"):
    T, K = a.shape
    N = w.shape[1]
    return pl.pallas_call(
        _matmul_kernel,
        out_shape=jax.ShapeDtypeStruct((T, N), out_dtype),
        grid=(T // tm, N // tn),
        in_specs=[pl.BlockSpec((tm, K), lambda i, j: (i, 0)),
                  pl.BlockSpec((K, tn), lambda i, j: (0, j))],
        out_specs=pl.BlockSpec((tm, tn), lambda i, j: (i, j)),
        compiler_params=_params(("parallel", "arbitrary")),
        name=name,
    )(a, w)


def _proj_idx_kernel(a_ref, w_ref, cs_ref, e_ref, c_ref, ik_ref, iw_ref):
    acc = _dot(a_ref[...], w_ref[...])
    kk = acc[:, 0:LANES]
    tab = _expand_tables(cs_ref, e_ref, c_ref, 3, 6)
    out = _rope_head(kk, tab[:, 0:LANES], tab[:, LANES:2 * LANES], tab[:, 2 * LANES:3 * LANES],
                     IDX_ROPE_DIM // 2)
    ik_ref[...] = out.astype(ik_ref.dtype)
    iw_ref[...] = acc[:, LANES:2 * LANES]


def _proj_idx(h, w, cs, e, c, tm=512):
    T, K = h.shape
    return pl.pallas_call(
        _proj_idx_kernel,
        out_shape=(jax.ShapeDtypeStruct((T, LANES), BF16), jax.ShapeDtypeStruct((T, LANES), F32)),
        grid=(T // tm,),
        in_specs=[pl.BlockSpec((tm, K), lambda i: (i, 0)),
                  pl.BlockSpec((K, 2 * LANES), lambda i: (0, 0)),
                  pl.BlockSpec((tm, 3 * ROT_SRC), lambda i: (i, 0)),
                  pl.BlockSpec((ROT_SRC, 6 * LANES), lambda i: (0, 0)),
                  pl.BlockSpec((1, 6 * LANES), lambda i: (0, 0))],
        out_specs=(pl.BlockSpec((tm, LANES), lambda i: (i, 0)),
                   pl.BlockSpec((tm, LANES), lambda i: (i, 0))),
        compiler_params=_params(("parallel",)),
        name="proj_idx",
    )(h, w, cs, e, c)


def _sb_kernel(q_ref, k_ref, v_ref, o_ref, acc_ref, carry_ref, *, t, nh, scale):
    qi = pl.program_id(2)
    tri = (lax.broadcasted_iota(jnp.int32, (t, t), 0)
           > lax.broadcasted_iota(jnp.int32, (t, t), 1)).astype(BF16)
    t_idx = qi * t + lax.broadcasted_iota(jnp.int32, (t, t), 0)
    col = lax.broadcasted_iota(jnp.int32, (t, t), 1)

    def block(h, off, carry):
        hs = slice(h * HEAD_DIM, (h + 1) * HEAD_DIM)
        z = _dot_nt(q_ref[:, hs], k_ref[pl.ds(off, t), hs]) * scale
        mask = (off + col) < t_idx
        e = jnp.exp(-jnp.abs(z))
        log_beta = jnp.minimum(z, 0.0) - jnp.log(1.0 + e)
        l1 = jnp.where(mask, log_beta - z, 0.0)
        hi = l1.astype(BF16)
        lo = (l1 - hi.astype(F32)).astype(BF16)
        suffix = _dot(hi, tri) + _dot(lo, tri)
        a = jnp.where(mask, jnp.exp(log_beta + suffix + carry), 0.0)
        return _dot(a.astype(BF16), v_ref[pl.ds(off, t), hs]), jnp.sum(l1, axis=1, keepdims=True)

    first = jnp.maximum(qi - 1, 0)
    off0 = pl.multiple_of(first * t, t)
    alive = jnp.int32(0)
    for h in range(nh):
        c_late, r_late = block(h, off0 + t, 0.0)
        c_early, r_early = block(h, off0, r_late)
        acc_ref[h] = c_late + c_early
        carry = r_late + r_early
        carry_ref[h] = carry
        alive = jnp.maximum(alive, (jnp.max(carry) > EXP_UNDERFLOW).astype(jnp.int32))

    def body(state):
        j, _ = state
        off = pl.multiple_of(j * t, t)
        alive = jnp.int32(0)
        for h in range(nh):
            carry = carry_ref[h]
            c, r = block(h, off, carry)
            acc_ref[h] += c
            carry = carry + r
            carry_ref[h] = carry
            alive = jnp.maximum(alive, (jnp.max(carry) > EXP_UNDERFLOW).astype(jnp.int32))
        return j - 1, alive

    lax.while_loop(lambda s: (s[0] >= 0) & (s[1] > 0), body, (first - 1, alive))
    for h in range(nh):
        o_ref[:, h * HEAD_DIM:(h + 1) * HEAD_DIM] = acc_ref[h].astype(o_ref.dtype)


def _sb_attention(pa, B, S, t=256, nh=4):
    G = N_SB_HEADS // nh
    w = nh * HEAD_DIM
    return pl.pallas_call(
        functools.partial(_sb_kernel, t=t, nh=nh, scale=HEAD_DIM ** -0.5),
        out_shape=jax.ShapeDtypeStruct((B, S, SB_W), BF16),
        grid=(B, G, S // t),
        in_specs=[pl.BlockSpec((None, t, w), lambda b, g, i: (b, i, g)),
                  pl.BlockSpec((None, S, w), lambda b, g, i: (b, 0, G + g)),
                  pl.BlockSpec((None, S, w), lambda b, g, i: (b, 0, 2 * G + g))],
        out_specs=pl.BlockSpec((None, t, w), lambda b, g, i: (b, i, g)),
        scratch_shapes=[pltpu.VMEM((nh, t, HEAD_DIM), F32), pltpu.VMEM((nh, t, 1), F32)],
        compiler_params=_params(("parallel", "parallel", "arbitrary")),
        name="sb_attention",
    )(pa, pa, pa)


PACKED_ROWS = 16
COUNT_CHAINS = 4


def _indexer_kernel(iq_ref, iw_ref, ik_ref, bias_ref, rhs_ref, hi_ref, mid_ref, lo_ref, sel_ref, *,
                    tq, tk, to, topk, n_kc, idx_scale):
    qi = pl.program_id(1)
    n_causal = ((qi + 1) * tq + tk - 1) // tk
    G = PACKED_ROWS
    n_groups = tk // G
    r_i = lax.broadcasted_iota(jnp.int32, (LANES, LANES), 0)
    c_i = lax.broadcasted_iota(jnp.int32, (LANES, LANES), 1)
    pick_lo = ((r_i == c_i) & (r_i < IDX_DIM)).astype(BF16)
    pick_hi = ((r_i == c_i) & (r_i >= IDX_DIM)).astype(BF16)
    for p in range(N_IDX_HEADS // 2):
        pair = iq_ref[:, p * LANES:(p + 1) * LANES]
        rhs_ref[2 * p] = _dot_nt(pick_lo, pair).astype(BF16)
        rhs_ref[2 * p + 1] = _dot_nt(pick_hi, pair).astype(BF16)
    w = (iw_ref[...] * idx_scale).T[0:N_IDX_HEADS, :]
    q_idx = qi * tq + lax.broadcasted_iota(jnp.int32, (tk, tq), 1)
    k_row = lax.broadcasted_iota(jnp.int32, (tk, tq), 0)

    def score_chunk(c, _):
        off = pl.multiple_of(c * tk, tk)
        kb = ik_ref[pl.ds(off, tk), :]
        score = jnp.zeros((tk, tq), F32)
        for h in range(N_IDX_HEADS):
            score = score + jnp.maximum(_dot(kb, rhs_ref[h]), 0.0) * w[h:h + 1, :]
        bits = pltpu.bitcast(score, jnp.int32)
        neg = bits < 0
        hi = pltpu.bitcast(bits & jnp.int32(-65536), F32)
        mid = (bits >> 8) & 0xFF
        lo = bits & 0xFF
        mid = jnp.where(neg, 255 - mid, mid)
        lo = jnp.where(neg, 255 - lo, lo)
        hi_ref[c] = jnp.where((off + k_row) <= q_idx, hi, -jnp.inf).astype(BF16)
        mid_ref[c] = mid.astype(F32).astype(BF16)
        lo_ref[c] = lo.astype(F32).astype(BF16)
        return 0

    lax.fori_loop(0, n_causal, score_chunk, 0)

    one = jnp.ones((G, tq), BF16)
    zero = jnp.zeros((G, tq), BF16)

    def tile16(row8):
        return jnp.concatenate([row8, row8], axis=0).astype(BF16)

    def count(v_ref, cand, strict=False):
        def body(c, tot):
            accs = [zero] * COUNT_CHAINS
            for g in range(n_groups):
                vs = v_ref[c, g * G:(g + 1) * G, :]
                hit = (vs > cand) if strict else (vs >= cand)
                accs[g % COUNT_CHAINS] = accs[g % COUNT_CHAINS] + jnp.where(hit, one, zero)
            acc = accs[0]
            for a in accs[1:]:
                acc = acc + a
            return tot + acc.astype(F32)
        tot = lax.fori_loop(0, n_causal, body, jnp.zeros((G, tq), F32))
        return jnp.broadcast_to(jnp.sum(tot, axis=0, keepdims=True), (8, tq))

    def key_to_f32(k):
        b16 = jnp.where(k >= 0, k, k ^ 0x7FFF) & 0xFFFF
        return pltpu.bitcast(b16 << 16, F32)

    def kth_int(v_ref, rank):
        def step(i, ans):
            cand = ans + (jnp.int32(128) >> i).astype(F32)
            return jnp.where(count(v_ref, tile16(cand)) >= rank, cand, ans)
        return tile16(lax.fori_loop(0, 8, step, jnp.zeros((8, tq), F32)))

    def restrict(src_ref, thr, dst_ref):
        def body(c, _):
            for g in range(n_groups):
                gs = slice(g * G, (g + 1) * G)
                dst_ref[c, gs, :] = jnp.where(src_ref[c, gs, :] == thr, dst_ref[c, gs, :], -one)
            return 0
        lax.fori_loop(0, n_causal, body, 0)

    t_q = qi * tq + lax.broadcasted_iota(jnp.int32, (8, tq), 1)
    keff = jnp.minimum(t_q + 1, topk).astype(F32)

    k0 = jnp.zeros((8, tq), jnp.int32)
    ans = jnp.where(count(hi_ref, tile16(key_to_f32(k0))) >= keff, k0, jnp.int32(-32768))

    def step1(i, ans):
        cand = ans | (jnp.int32(1) << (14 - i))
        return jnp.where(count(hi_ref, tile16(key_to_f32(cand))) >= keff, cand, ans)

    t_hi = tile16(key_to_f32(lax.fori_loop(0, 15, step1, ans)))
    r1 = keff - count(hi_ref, t_hi, strict=True)
    restrict(hi_ref, t_hi, mid_ref)
    t_mid = kth_int(mid_ref, r1)
    r2 = r1 - count(mid_ref, t_mid, strict=True)
    restrict(mid_ref, t_mid, lo_ref)
    zero8 = jnp.zeros((8, tq), F32)

    def unresolved(cnt):
        return jnp.max(jnp.where(cnt == r2, 0.0, 1.0)).astype(jnp.int32)

    cnt0 = count(lo_ref, tile16(zero8))

    def step3(state):
        i, ans, cnt_ans, _ = state
        cand = ans + (jnp.int32(128) >> i).astype(F32)
        cnt = count(lo_ref, tile16(cand))
        ok = cnt >= r2
        ans = jnp.where(ok, cand, ans)
        cnt_ans = jnp.where(ok, cnt, cnt_ans)
        return i + 1, ans, cnt_ans, unresolved(cnt_ans)

    _, ans3, _, _ = lax.while_loop(lambda s: (s[0] < 8) & (s[3] > 0), step3,
                                   (jnp.int32(0), zero8, cnt0, unresolved(cnt0)))
    t_lo = tile16(ans3)

    eye = (lax.broadcasted_iota(jnp.int32, (tq, tq), 0)
           == lax.broadcasted_iota(jnp.int32, (tq, tq), 1)).astype(BF16)

    def write_chunk(c, _):
        for g in range(n_groups):
            gs = slice(g * G, (g + 1) * G)
            sel = ((hi_ref[c, gs, :] > t_hi) | (mid_ref[c, gs, :] > t_mid) | (lo_ref[c, gs, :] >= t_lo))
            sel_ref[gs, :] = jnp.where(sel, one, zero)
        sel_t = _dot_nt(eye, sel_ref[...])
        bias = jnp.where(sel_t > 0.5, 0.0, -jnp.inf).astype(bias_ref.dtype)
        for s in range(tk // to):
            bias_ref[c * (tk // to) + s] = bias[:, s * to:(s + 1) * to]
        return 0

    lax.fori_loop(0, n_causal, write_chunk, 0)

    def fill_chunk(c, _):
        bias_ref[c] = jnp.full((tq, to), -jnp.inf, bias_ref.dtype)
        return 0

    lax.fori_loop(n_causal * (tk // to), n_kc * (tk // to), fill_chunk, 0)


def _indexer(pa, iw, ik2, B, S, topk, tq=256, tk=1024, to=512):
    n_kc = S // tk
    assert tk // PACKED_ROWS <= 256
    field = pltpu.VMEM((n_kc, tk, tq), BF16)
    return pl.pallas_call(
        functools.partial(_indexer_kernel, tq=tq, tk=tk, to=to, topk=topk, n_kc=n_kc,
                          idx_scale=IDX_DIM ** -0.5 * N_IDX_HEADS ** -0.5),
        out_shape=jax.ShapeDtypeStruct((B, S // to, S, to), BF16),
        grid=(B, S // tq),
        in_specs=[pl.BlockSpec((None, tq, IDX_QW), lambda b, i: (b, i, 6)),
                  pl.BlockSpec((None, tq, LANES), lambda b, i: (b, i, 0)),
                  pl.BlockSpec((None, S, LANES), lambda b, i: (b, 0, 0))],
        out_specs=pl.BlockSpec((None, S // to, tq, to), lambda b, i: (b, 0, i, 0)),
        scratch_shapes=[pltpu.VMEM((N_IDX_HEADS, LANES, tq), BF16), field, field, field,
                        pltpu.VMEM((tk, tq), BF16)],
        compiler_params=_params(("parallel", "arbitrary")),
        name="indexer",
    )(pa, iw, ik2)


def _dsa_kernel(qt_ref, kt_ref, q_ref, k_ref, v_ref, m_ref, o_ref, acc_ref, mx_ref, vaug_ref, bias_ref, *,
                t, scale):
    qi = qt_ref[pl.program_id(1)]
    kj = kt_ref[pl.program_id(1)]
    reps = t // LANES

    @pl.when(kj == 0)
    def _():
        acc_ref[...] = jnp.zeros_like(acc_ref)
        mx_ref[...] = jnp.full_like(mx_ref, NEG_BIG)
        vaug_ref[:, :, HEAD_DIM:] = jnp.ones((N_DSA_HEADS, t, LANES), BF16)

    bias_ref[...] = m_ref[...].astype(F32)
    for h in range(N_DSA_HEADS):
        hs = slice(h * HEAD_DIM, (h + 1) * HEAD_DIM)
        vaug_ref[h, :, 0:HEAD_DIM] = v_ref[:, hs]
        s = _dot_nt(q_ref[:, hs], k_ref[:, hs]) * (scale * LOG2_E) + bias_ref[...]
        m_old = mx_ref[h]
        m_new = jnp.maximum(m_old, jnp.max(s, axis=1, keepdims=True))
        alpha = jnp.exp2(m_old - m_new)
        p = jnp.exp2(s - jnp.tile(m_new, (1, reps)))
        acc_ref[h] = jnp.tile(alpha, (1, 2)) * acc_ref[h] + _dot(p.astype(BF16), vaug_ref[h])
        mx_ref[h] = m_new

    @pl.when(kj == qi)
    def _():
        for h in range(N_DSA_HEADS):
            o_ref[:, h * HEAD_DIM:(h + 1) * HEAD_DIM] = (
                acc_ref[h, :, 0:HEAD_DIM] / acc_ref[h, :, HEAD_DIM:]).astype(o_ref.dtype)


def _dsa_attention(pa, bias, B, S, t=512):
    n = S // t
    pairs = [(i, j) for i in range(n) for j in range(i + 1)]
    qt = jnp.asarray([p[0] for p in pairs], jnp.int32)
    kt = jnp.asarray([p[1] for p in pairs], jnp.int32)
    grid_spec = pltpu.PrefetchScalarGridSpec(
        num_scalar_prefetch=2,
        grid=(B, len(pairs)),
        in_specs=[pl.BlockSpec((None, t, DSA_W), lambda b, p, qt, kt: (b, qt[p], 3)),
                  pl.BlockSpec((None, t, DSA_W), lambda b, p, qt, kt: (b, kt[p], 4)),
                  pl.BlockSpec((None, t, DSA_W), lambda b, p, qt, kt: (b, kt[p], 5)),
                  pl.BlockSpec((None, None, t, t), lambda b, p, qt, kt: (b, kt[p], qt[p], 0))],
        out_specs=pl.BlockSpec((None, t, DSA_W), lambda b, p, qt, kt: (b, qt[p], 0)),
        scratch_shapes=[pltpu.VMEM((N_DSA_HEADS, t, 2 * HEAD_DIM), F32),
                        pltpu.VMEM((N_DSA_HEADS, t, LANES), F32),
                        pltpu.VMEM((N_DSA_HEADS, t, 2 * HEAD_DIM), BF16),
                        pltpu.VMEM((t, t), F32)])
    return pl.pallas_call(
        functools.partial(_dsa_kernel, t=t, scale=HEAD_DIM ** -0.5),
        out_shape=jax.ShapeDtypeStruct((B, S, DSA_W), BF16),
        grid_spec=grid_spec,
        compiler_params=_params(("parallel", "arbitrary")),
        name="dsa_attention",
    )(qt, kt, pa, pa, pa, bias)


def _merge_kernel(osb_ref, ods_ref, wsb_ref, wds_ref, gsb_ref, gds_ref, o_ref):
    a = _dot(osb_ref[...], wsb_ref[...])
    b = _dot(ods_ref[...], wds_ref[...])
    o_ref[...] = (jax.nn.sigmoid(gsb_ref[...]) * a + jax.nn.sigmoid(gds_ref[...]) * b).astype(o_ref.dtype)


def _merge(o_sb, o_ds, w_branch, gates, tm=1024, tn=1024):
    T = o_sb.shape[0]
    nj = D_MODEL // tn
    return pl.pallas_call(
        _merge_kernel,
        out_shape=jax.ShapeDtypeStruct((T, D_MODEL), BF16),
        grid=(T // tm, nj),
        in_specs=[pl.BlockSpec((tm, SB_W), lambda i, j: (i, 0)),
                  pl.BlockSpec((tm, DSA_W), lambda i, j: (i, 0)),
                  pl.BlockSpec((SB_W, tn), lambda i, j: (0, j)),
                  pl.BlockSpec((DSA_W, tn), lambda i, j: (SB_W // DSA_W, j)),
                  pl.BlockSpec((tm, tn), lambda i, j: (i, j)),
                  pl.BlockSpec((tm, tn), lambda i, j: (i, nj + j))],
        out_specs=pl.BlockSpec((tm, tn), lambda i, j: (i, j)),
        compiler_params=_params(("parallel", "arbitrary")),
        name="merge",
    )(o_sb, o_ds, w_branch, w_branch, gates, gates)


def _matmul_res_kernel(a_ref, w_ref, r_ref, o_ref):
    o_ref[...] = r_ref[...] + _dot(a_ref[...], w_ref[...])


def _matmul_res(a, w, res, tm=1024, tn=1024, name="matmul_res"):
    T, K = a.shape
    N = w.shape[1]
    return pl.pallas_call(
        _matmul_res_kernel,
        out_shape=jax.ShapeDtypeStruct((T, N), F32),
        grid=(T // tm, N // tn),
        in_specs=[pl.BlockSpec((tm, K), lambda i, j: (i, 0)),
                  pl.BlockSpec((K, tn), lambda i, j: (0, j)),
                  pl.BlockSpec((tm, tn), lambda i, j: (i, j))],
        out_specs=pl.BlockSpec((tm, tn), lambda i, j: (i, j)),
        compiler_params=_params(("parallel", "arbitrary")),
        name=name,
    )(a, w, res)


HALO = PACKED_ROWS


def _ffn_kernel(x_ref, xp_ref, g_ref, wg_ref, wv_ref, cwg_ref, cwv_ref, cbg_ref, cbv_ref, wd_ref, o_ref,
                h_ref, *, tm, rows_per_seq):
    i = pl.program_id(0)
    j = pl.program_id(1)

    def norm(x):
        ms = jnp.mean(x * x, axis=-1, keepdims=True)
        return (x * lax.rsqrt(ms + EPS) * g_ref[...]).astype(BF16)

    @pl.when(j == 0)
    def _():
        x = x_ref[...]
        h_ref[HALO:, :] = norm(x)
        first = (i % (rows_per_seq // tm)) == 0
        hp = norm(xp_ref[...])
        h_ref[0:HALO, :] = jnp.where(first, jnp.zeros_like(hp), hp)
        o_ref[...] = x

    h = h_ref[...]

    def conv(w_ref, cw_ref, cb_ref):
        u = _dot(h, w_ref[...])
        u0 = u[HALO:, :]
        u1 = pltpu.roll(u, 1, 0)[HALO:, :]
        u2 = pltpu.roll(u, 2, 0)[HALO:, :]
        return cw_ref[0:1, :] * u2 + cw_ref[1:2, :] * u1 + cw_ref[2:3, :] * u0 + cb_ref[...]

    gate = conv(wg_ref, cwg_ref, cbg_ref)
    val = conv(wv_ref, cwv_ref, cbv_ref)
    act = (gate * jax.nn.sigmoid(gate) * val).astype(BF16)
    o_ref[...] += _dot(act, wd_ref[...])


def _ffn(x1, g, w_up, conv_w, conv_b, w_down, rows_per_seq, tm=512, tf=512):
    T, D = x1.shape
    nj = D_FF // tf
    cb = conv_b.reshape(1, 2 * D_FF)
    return pl.pallas_call(
        functools.partial(_ffn_kernel, tm=tm, rows_per_seq=rows_per_seq),
        out_shape=jax.ShapeDtypeStruct((T, D), F32),
        grid=(T // tm, nj),
        in_specs=[pl.BlockSpec((tm, D), lambda i, j: (i, 0)),
                  pl.BlockSpec((HALO, D), lambda i, j: (jnp.maximum(i * (tm // HALO) - 1, 0), 0)),
                  pl.BlockSpec((1, D), lambda i, j: (0, 0)),
                  pl.BlockSpec((D, tf), lambda i, j: (0, j)),
                  pl.BlockSpec((D, tf), lambda i, j: (0, nj + j)),
                  pl.BlockSpec((CONV_WIDTH, tf), lambda i, j: (0, j)),
                  pl.BlockSpec((CONV_WIDTH, tf), lambda i, j: (0, nj + j)),
                  pl.BlockSpec((1, tf), lambda i, j: (0, j)),
                  pl.BlockSpec((1, tf), lambda i, j: (0, nj + j)),
                  pl.BlockSpec((tf, D), lambda i, j: (j, 0))],
        out_specs=pl.BlockSpec((tm, D), lambda i, j: (i, 0)),
        scratch_shapes=[pltpu.VMEM((HALO + tm, D), BF16)],
        compiler_params=_params(("parallel", "arbitrary")),
        name="ffn",
    )(x1, x1, g.reshape(1, D), w_up, w_up, conv_w, conv_w, cb, cb, w_down)


def _rope_expansion():
    e = np.zeros((ROT_SRC, 6 * LANES), np.float32)
    c = np.zeros((1, 6 * LANES), np.float32)
    src = 0
    for t0, (rot, width) in ((0, (ROPE_DIM, HEAD_DIM)), (3, (IDX_ROPE_DIM, IDX_DIM))):
        half = rot // 2
        cos0, sin0 = src, src + half
        src += rot
        for g in range(LANES // width):
            base = g * width
            for i in range(half):
                e[cos0 + i, t0 * LANES + base + i] = 1.0
                e[cos0 + i, t0 * LANES + base + half + i] = 1.0
                e[sin0 + i, (t0 + 1) * LANES + base + half + i] = 1.0
                e[sin0 + i, (t0 + 2) * LANES + base + i] = -1.0
            c[0, t0 * LANES + base + rot:t0 * LANES + base + width] = 1.0
    return jnp.asarray(e, BF16), jnp.asarray(c, F32)


def _rope_compact(positions):
    pos = positions.reshape(-1).astype(F32)[:, None]
    is_cos = np.zeros((ROT_SRC,), bool)
    lane_freq = []
    src = 0
    for rot in (ROPE_DIM, IDX_ROPE_DIM):
        f = ROPE_THETA ** (-jnp.arange(0, rot, 2, dtype=F32) / rot)
        lane_freq += [f, f]
        is_cos[src:src + rot // 2] = True
        src += rot
    lane_freq = jnp.concatenate(lane_freq + [jnp.zeros((ROT_SRC - src,), F32)])
    ang = pos * lane_freq[None, :]
    cs = jnp.where(jnp.asarray(is_cos)[None, :], jnp.cos(ang), jnp.sin(ang))

    def top16(v):
        return lax.bitcast_convert_type(lax.bitcast_convert_type(v, jnp.int32) & jnp.int32(-65536), F32)

    hi = top16(cs)
    r = cs - hi
    mid = top16(r)
    lo = r - mid
    return jnp.concatenate([hi, mid, lo], axis=1).astype(BF16)


def kernel(x, positions, attn_norm_g, w_in, q_norm_g, k_norm_g, w_branch, w_out, ffn_norm_g, w_up,
           conv_w, conv_b, w_down):
    B, S, D = x.shape
    assert D == D_MODEL and S % 1024 == 0
    T = B * S
    topk = min(MAX_TOPK, S // 4)
    xf = x.reshape(T, D)

    o_qix = 3 * SB_W + 3 * DSA_W
    o_kix = o_qix + IDX_QW
    o_wix = o_kix + IDX_DIM
    o_g = o_wix + N_IDX_HEADS
    w_in_bf = w_in.astype(BF16)
    w_kix = w_in_bf[:, o_kix:o_wix]
    w_idx = jnp.concatenate([w_kix, w_kix, w_in_bf[:, o_wix:o_g],
                             jnp.zeros((D, LANES - N_IDX_HEADS), BF16)], axis=1)
    w_gate = w_in[:, o_g:].astype(BF16)

    cs = _rope_compact(positions)
    rope_e, rope_c = _rope_expansion()

    h1 = _rmsnorm(xf, attn_norm_g)
    pa = _proj_attn(h1, w_in_bf, o_kix, cs, rope_e, rope_c, q_norm_g, k_norm_g)
    gates = _matmul(h1, w_gate, F32, name="proj_gates")
    ik2, iw = _proj_idx(h1, w_idx, cs, rope_e, rope_c)

    pa3 = pa.reshape(B, S, pa.shape[1])
    o_sb = _sb_attention(pa3, B, S)
    bias = _indexer(pa3, iw.reshape(B, S, LANES), ik2.reshape(B, S, LANES), B, S, topk)
    o_ds = _dsa_attention(pa3, bias, B, S)

    merged = _merge(o_sb.reshape(T, SB_W), o_ds.reshape(T, DSA_W), w_branch.astype(BF16), gates)
    x1 = _matmul_res(merged, w_out.astype(BF16), xf, name="out_proj")

    out = _ffn(x1, ffn_norm_g, w_up.astype(BF16), conv_w, conv_b, w_down.astype(BF16), S)
    return out.reshape(B, S, D)
```

```python
import functools

import jax
import jax.numpy as jnp
import numpy as np
from jax import lax
from jax.experimental import pallas as pl
from jax.experimental.pallas import tpu as pltpu

D_MODEL = 2048
HEAD_DIM = 128
N_SB_HEADS = 8
N_DSA_HEADS = 8
N_IDX_HEADS = 16
IDX_DIM = 64
ROPE_THETA = 500000.0
ROPE_DIM = HEAD_DIM // 4
IDX_ROPE_DIM = IDX_DIM // 4
MAX_TOPK = 256
D_FF = 5632
CONV_WIDTH = 3
EPS = 1e-6

SB_W = N_SB_HEADS * HEAD_DIM
DSA_W = N_DSA_HEADS * HEAD_DIM
IDX_QW = N_IDX_HEADS * IDX_DIM

LANES = 128
VMEM_LIMIT = 56 * 1024 * 1024
NEG_BIG = -1e30
EXP_UNDERFLOW = -105.0
LOG2_E = 1.4426950408889634

F32 = jnp.float32
BF16 = jnp.bfloat16


def _params(semantics):
    return pltpu.CompilerParams(dimension_semantics=semantics, vmem_limit_bytes=VMEM_LIMIT)


def _dot(a, b):
    return jnp.dot(a, b, preferred_element_type=F32)


def _dot_nt(a, b):
    return lax.dot_general(a, b, (((1,), (1,)), ((), ())), preferred_element_type=F32)


def _rope_head(xh, cos_t, sin_up, sin_dn, half):
    return (xh * cos_t + pltpu.roll(xh, half, 1) * sin_up
            + pltpu.roll(xh, LANES - half, 1) * sin_dn)


ROT_SRC = LANES


def _expand_tables(cs_ref, e_ref, c_ref, lo, hi):
    e = e_ref[:, lo * LANES:hi * LANES]
    out = c_ref[:, lo * LANES:hi * LANES]
    for p in range(3):
        out = out + _dot(cs_ref[:, p * ROT_SRC:(p + 1) * ROT_SRC], e)
    return out


def _proj_attn_kernel(x_ref, ng_ref, w_ref, cs_ref, e_ref, c_ref, qg_ref, kg_ref, o_ref, h_ref, *, tn):
    @pl.when(pl.program_id(1) == 0)
    def _():
        x = x_ref[...]
        ms = jnp.mean(x * x, axis=-1, keepdims=True)
        h_ref[...] = (x * lax.rsqrt(ms + EPS) * ng_ref[...]).astype(h_ref.dtype)

    acc = _dot(h_ref[...], w_ref[...])
    region = pl.program_id(1) // (SB_W // tn)
    heads = tn // LANES

    def norm_rope(g_ref):
        tab = _expand_tables(cs_ref, e_ref, c_ref, 0, 3)
        for h in range(heads):
            xh = acc[:, h * LANES:(h + 1) * LANES]
            ms = jnp.mean(xh * xh, axis=-1, keepdims=True)
            xh = xh * lax.rsqrt(ms + EPS) * g_ref[...]
            out = _rope_head(xh, tab[:, 0:LANES], tab[:, LANES:2 * LANES],
                             tab[:, 2 * LANES:3 * LANES], ROPE_DIM // 2)
            o_ref[:, h * LANES:(h + 1) * LANES] = out.astype(o_ref.dtype)

    @pl.when((region <= 2) | (region == 5))
    def _():
        o_ref[...] = acc.astype(o_ref.dtype)

    @pl.when(region == 3)
    def _():
        norm_rope(qg_ref)

    @pl.when(region == 4)
    def _():
        norm_rope(kg_ref)

    @pl.when(region == 6)
    def _():
        tab = _expand_tables(cs_ref, e_ref, c_ref, 3, 6)
        for h in range(heads):
            xh = acc[:, h * LANES:(h + 1) * LANES]
            out = _rope_head(xh, tab[:, 0:LANES], tab[:, LANES:2 * LANES],
                             tab[:, 2 * LANES:3 * LANES], IDX_ROPE_DIM // 2)
            o_ref[:, h * LANES:(h + 1) * LANES] = out.astype(o_ref.dtype)


def _proj_attn(x, ng, w, N, cs, e, c, qg, kg, tm=1024, tn=1024):
    T, K = x.shape
    return pl.pallas_call(
        functools.partial(_proj_attn_kernel, tn=tn),
        out_shape=(jax.ShapeDtypeStruct((T, N), BF16), jax.ShapeDtypeStruct((T, K), BF16)),
        grid=(T // tm, N // tn),
        in_specs=[pl.BlockSpec((tm, K), lambda i, j: (i, 0)),
                  pl.BlockSpec((1, K), lambda i, j: (0, 0)),
                  pl.BlockSpec((K, tn), lambda i, j: (0, j)),
                  pl.BlockSpec((tm, 3 * ROT_SRC), lambda i, j: (i, 0)),
                  pl.BlockSpec((ROT_SRC, 6 * LANES), lambda i, j: (0, 0)),
                  pl.BlockSpec((1, 6 * LANES), lambda i, j: (0, 0)),
                  pl.BlockSpec((1, LANES), lambda i, j: (0, 0)),
                  pl.BlockSpec((1, LANES), lambda i, j: (0, 0))],
        out_specs=(pl.BlockSpec((tm, tn), lambda i, j: (i, j)),
                   pl.BlockSpec((tm, K), lambda i, j: (i, 0))),
        compiler_params=_params(("parallel", "arbitrary")),
        name="proj_attn",
    )(x, ng.reshape(1, K), w, cs, e, c, qg.reshape(1, LANES), kg.reshape(1, LANES))


def _matmul_kernel(a_ref, w_ref, o_ref):
    o_ref[...] = _dot(a_ref[...], w_ref[...]).astype(o_ref.dtype)


def _matmul(a, w, out_dtype, tm=1024, tn=1024, name="matmul"):
    T, K = a.shape
    N = w.shape[1]
    return pl.pallas_call(
        _matmul_kernel,
        out_shape=jax.ShapeDtypeStruct((T, N), out_dtype),
        grid=(T // tm, N // tn),
        in_specs=[pl.BlockSpec((tm, K), lambda i, j: (i, 0)),
                  pl.BlockSpec((K, tn), lambda i, j: (0, j))],
        out_specs=pl.BlockSpec((tm, tn), lambda i, j: (i, j)),
        compiler_params=_params(("parallel", "arbitrary")),
        name=name,
    )(a, w)


def _proj_idx_kernel(a_ref, w_ref, cs_ref, e_ref, c_ref, ik_ref, iw_ref):
    acc = _dot(a_ref[...], w_ref[...])
    kk = acc[:, 0:LANES]
    tab = _expand_tables(cs_ref, e_ref, c_ref, 3, 6)
    out = _rope_head(kk, tab[:, 0:LANES], tab[:, LANES:2 * LANES], tab[:, 2 * LANES:3 * LANES],
                     IDX_ROPE_DIM // 2)
    ik_ref[...] = out.astype(ik_ref.dtype)
    iw_ref[...] = acc[:, LANES:2 * LANES]


def _proj_idx(h, w, cs, e, c, tm=512):
    T, K = h.shape
    return pl.pallas_call(
        _proj_idx_kernel,
        out_shape=(jax.ShapeDtypeStruct((T, LANES), BF16), jax.ShapeDtypeStruct((T, LANES), F32)),
        grid=(T // tm,),
        in_specs=[pl.BlockSpec((tm, K), lambda i: (i, 0)),
                  pl.BlockSpec((K, 2 * LANES), lambda i: (0, 0)),
                  pl.BlockSpec((tm, 3 * ROT_SRC), lambda i: (i, 0)),
                  pl.BlockSpec((ROT_SRC, 6 * LANES), lambda i: (0, 0)),
                  pl.BlockSpec((1, 6 * LANES), lambda i: (0, 0))],
        out_specs=(pl.BlockSpec((tm, LANES), lambda i: (i, 0)),
                   pl.BlockSpec((tm, LANES), lambda i: (i, 0))),
        compiler_params=_params(("parallel",)),
        name="proj_idx",
    )(h, w, cs, e, c)


def _sb_kernel(q_ref, k_ref, v_ref, o_ref, acc_ref, carry_ref, *, t, nh, scale):
    qi = pl.program_id(2)
    tri = (lax.broadcasted_iota(jnp.int32, (t, t), 0)
           > lax.broadcasted_iota(jnp.int32, (t, t), 1)).astype(BF16)
    t_idx = qi * t + lax.broadcasted_iota(jnp.int32, (t, t), 0)
    col = lax.broadcasted_iota(jnp.int32, (t, t), 1)

    def block(h, off, carry):
        hs = slice(h * HEAD_DIM, (h + 1) * HEAD_DIM)
        z = _dot_nt(q_ref[:, hs], k_ref[pl.ds(off, t), hs]) * scale
        mask = (off + col) < t_idx
        e = jnp.exp(-jnp.abs(z))
        log_beta = jnp.minimum(z, 0.0) - jnp.log(1.0 + e)
        l1 = jnp.where(mask, log_beta - z, 0.0)
        hi = l1.astype(BF16)
        lo = (l1 - hi.astype(F32)).astype(BF16)
        suffix = _dot(hi, tri) + _dot(lo, tri)
        a = jnp.where(mask, jnp.exp(log_beta + suffix + carry), 0.0)
        return _dot(a.astype(BF16), v_ref[pl.ds(off, t), hs]), jnp.sum(l1, axis=1, keepdims=True)

    first = jnp.maximum(qi - 1, 0)
    off0 = pl.multiple_of(first * t, t)
    alive = jnp.int32(0)
    for h in range(nh):
        c_late, r_late = block(h, off0 + t, 0.0)
        c_early, r_early = block(h, off0, r_late)
        acc_ref[h] = c_late + c_early
        carry = r_late + r_early
        carry_ref[h] = carry
        alive = jnp.maximum(alive, (jnp.max(carry) > EXP_UNDERFLOW).astype(jnp.int32))

    def body(state):
        j, _ = state
        off = pl.multiple_of(j * t, t)
        alive = jnp.int32(0)
        for h in range(nh):
            carry = carry_ref[h]
            c, r = block(h, off, carry)
            acc_ref[h] += c
            carry = carry + r
            carry_ref[h] = carry
            alive = jnp.maximum(alive, (jnp.max(carry) > EXP_UNDERFLOW).astype(jnp.int32))
        return j - 1, alive

    lax.while_loop(lambda s: (s[0] >= 0) & (s[1] > 0), body, (first - 1, alive))
    for h in range(nh):
        o_ref[:, h * HEAD_DIM:(h + 1) * HEAD_DIM] = acc_ref[h].astype(o_ref.dtype)


def _sb_attention(pa, B, S, t=256, nh=4):
    G = N_SB_HEADS // nh
    w = nh * HEAD_DIM
    return pl.pallas_call(
        functools.partial(_sb_kernel, t=t, nh=nh, scale=HEAD_DIM ** -0.5),
        out_shape=jax.ShapeDtypeStruct((B, S, SB_W), BF16),
        grid=(B, G, S // t),
        in_specs=[pl.BlockSpec((None, t, w), lambda b, g, i: (b, i, g)),
                  pl.BlockSpec((None, S, w), lambda b, g, i: (b, 0, G + g)),
                  pl.BlockSpec((None, S, w), lambda b, g, i: (b, 0, 2 * G + g))],
        out_specs=pl.BlockSpec((None, t, w), lambda b, g, i: (b, i, g)),
        scratch_shapes=[pltpu.VMEM((nh, t, HEAD_DIM), F32), pltpu.VMEM((nh, t, 1), F32)],
        compiler_params=_params(("parallel", "parallel", "arbitrary")),
        name="sb_attention",
    )(pa, pa, pa)


PACKED_ROWS = 16
COUNT_CHAINS = 4


def _indexer_kernel(iq_ref, iw_ref, ik_ref, bias_ref, rhs_ref, hi_ref, mid_ref, lo_ref, sel_ref, *,
                    tq, tk, to, topk, n_kc, idx_scale):
    qi = pl.program_id(1)
    n_causal = ((qi + 1) * tq + tk - 1) // tk
    G = PACKED_ROWS
    n_groups = tk // G
    r_i = lax.broadcasted_iota(jnp.int32, (LANES, LANES), 0)
    c_i = lax.broadcasted_iota(jnp.int32, (LANES, LANES), 1)
    pick_lo = ((r_i == c_i) & (r_i < IDX_DIM)).astype(BF16)
    pick_hi = ((r_i == c_i) & (r_i >= IDX_DIM)).astype(BF16)
    for p in range(N_IDX_HEADS // 2):
        pair = iq_ref[:, p * LANES:(p + 1) * LANES]
        rhs_ref[2 * p] = _dot_nt(pick_lo, pair).astype(BF16)
        rhs_ref[2 * p + 1] = _dot_nt(pick_hi, pair).astype(BF16)
    w = (iw_ref[...] * idx_scale).T[0:N_IDX_HEADS, :]
    q_idx = qi * tq + lax.broadcasted_iota(jnp.int32, (tk, tq), 1)
    k_row = lax.broadcasted_iota(jnp.int32, (tk, tq), 0)

    def score_chunk(c, _):
        off = pl.multiple_of(c * tk, tk)
        kb = ik_ref[pl.ds(off, tk), :]
        score = jnp.zeros((tk, tq), F32)
        for h in range(N_IDX_HEADS):
            score = score + jnp.maximum(_dot(kb, rhs_ref[h]), 0.0) * w[h:h + 1, :]
        bits = pltpu.bitcast(score, jnp.int32)
        neg = bits < 0
        hi = pltpu.bitcast(bits & jnp.int32(-65536), F32)
        mid = (bits >> 8) & 0xFF
        lo = bits & 0xFF
        mid = jnp.where(neg, 255 - mid, mid)
        lo = jnp.where(neg, 255 - lo, lo)
        hi_ref[c] = jnp.where((off + k_row) <= q_idx, hi, -jnp.inf).astype(BF16)
        mid_ref[c] = mid.astype(F32).astype(BF16)
        lo_ref[c] = lo.astype(F32).astype(BF16)
        return 0

    lax.fori_loop(0, n_causal, score_chunk, 0)

    one = jnp.ones((G, tq), BF16)
    zero = jnp.zeros((G, tq), BF16)

    def tile16(row8):
        return jnp.concatenate([row8, row8], axis=0).astype(BF16)

    def count(v_ref, cand, strict=False):
        def body(c, tot):
            accs = [zero] * COUNT_CHAINS
            for g in range(n_groups):
                vs = v_ref[c, g * G:(g + 1) * G, :]
                hit = (vs > cand) if strict else (vs >= cand)
                accs[g % COUNT_CHAINS] = accs[g % COUNT_CHAINS] + jnp.where(hit, one, zero)
            acc = accs[0]
            for a in accs[1:]:
                acc = acc + a
            return tot + acc.astype(F32)
        tot = lax.fori_loop(0, n_causal, body, jnp.zeros((G, tq), F32))
        return jnp.broadcast_to(jnp.sum(tot, axis=0, keepdims=True), (8, tq))

    def key_to_f32(k):
        b16 = jnp.where(k >= 0, k, k ^ 0x7FFF) & 0xFFFF
        return pltpu.bitcast(b16 << 16, F32)

    def kth_int(v_ref, rank):
        def step(i, ans):
            cand = ans + (jnp.int32(128) >> i).astype(F32)
            return jnp.where(count(v_ref, tile16(cand)) >= rank, cand, ans)
        return tile16(lax.fori_loop(0, 8, step, jnp.zeros((8, tq), F32)))

    def restrict(src_ref, thr, dst_ref):
        def body(c, _):
            for g in range(n_groups):
                gs = slice(g * G, (g + 1) * G)
                dst_ref[c, gs, :] = jnp.where(src_ref[c, gs, :] == thr, dst_ref[c, gs, :], -one)
            return 0
        lax.fori_loop(0, n_causal, body, 0)

    t_q = qi * tq + lax.broadcasted_iota(jnp.int32, (8, tq), 1)
    keff = jnp.minimum(t_q + 1, topk).astype(F32)

    k0 = jnp.zeros((8, tq), jnp.int32)
    ans = jnp.where(count(hi_ref, tile16(key_to_f32(k0))) >= keff, k0, jnp.int32(-32768))

    def step1(i, ans):
        cand = ans | (jnp.int32(1) << (14 - i))
        return jnp.where(count(hi_ref, tile16(key_to_f32(cand))) >= keff, cand, ans)

    t_hi = tile16(key_to_f32(lax.fori_loop(0, 15, step1, ans)))
    r1 = keff - count(hi_ref, t_hi, strict=True)
    restrict(hi_ref, t_hi, mid_ref)
    t_mid = kth_int(mid_ref, r1)
    r2 = r1 - count(mid_ref, t_mid, strict=True)
    restrict(mid_ref, t_mid, lo_ref)
    zero8 = jnp.zeros((8, tq), F32)

    def unresolved(cnt):
        return jnp.max(jnp.where(cnt == r2, 0.0, 1.0)).astype(jnp.int32)

    cnt0 = count(lo_ref, tile16(zero8))

    def step3(state):
        i, ans, cnt_ans, _ = state
        cand = ans + (jnp.int32(128) >> i).astype(F32)
        cnt = count(lo_ref, tile16(cand))
        ok = cnt >= r2
        ans = jnp.where(ok, cand, ans)
        cnt_ans = jnp.where(ok, cnt, cnt_ans)
        return i + 1, ans, cnt_ans, unresolved(cnt_ans)

    _, ans3, _, _ = lax.while_loop(lambda s: (s[0] < 8) & (s[3] > 0), step3,
                                   (jnp.int32(0), zero8, cnt0, unresolved(cnt0)))
    t_lo = tile16(ans3)

    eye = (lax.broadcasted_iota(jnp.int32, (tq, tq), 0)
           == lax.broadcasted_iota(jnp.int32, (tq, tq), 1)).astype(BF16)

    def write_chunk(c, _):
        for g in range(n_groups):
            gs = slice(g * G, (g + 1) * G)
            sel = ((hi_ref[c, gs, :] > t_hi) | (mid_ref[c, gs, :] > t_mid) | (lo_ref[c, gs, :] >= t_lo))
            sel_ref[gs, :] = jnp.where(sel, one, zero)
        sel_t = _dot_nt(eye, sel_ref[...])
        bias = jnp.where(sel_t > 0.5, 0.0, -jnp.inf).astype(bias_ref.dtype)
        for s in range(tk // to):
            bias_ref[c * (tk // to) + s] = bias[:, s * to:(s + 1) * to]
        return 0

    lax.fori_loop(0, n_causal, write_chunk, 0)

    def fill_chunk(c, _):
        bias_ref[c] = jnp.full((tq, to), -jnp.inf, bias_ref.dtype)
        return 0

    lax.fori_loop(n_causal * (tk // to), n_kc * (tk // to), fill_chunk, 0)


def _indexer(pa, iw, ik2, B, S, topk, tq=256, tk=1024, to=512):
    n_kc = S // tk
    assert tk // PACKED_ROWS <= 256
    field = pltpu.VMEM((n_kc, tk, tq), BF16)
    return pl.pallas_call(
        functools.partial(_indexer_kernel, tq=tq, tk=tk, to=to, topk=topk, n_kc=n_kc,
                          idx_scale=IDX_DIM ** -0.5 * N_IDX_HEADS ** -0.5),
        out_shape=jax.ShapeDtypeStruct((B, S // to, S, to), BF16),
        grid=(B, S // tq),
        in_specs=[pl.BlockSpec((None, tq, IDX_QW), lambda b, i: (b, i, 6)),
                  pl.BlockSpec((None, tq, LANES), lambda b, i: (b, i, 0)),
                  pl.BlockSpec((None, S, LANES), lambda b, i: (b, 0, 0))],
        out_specs=pl.BlockSpec((None, S // to, tq, to), lambda b, i: (b, 0, i, 0)),
        scratch_shapes=[pltpu.VMEM((N_IDX_HEADS, LANES, tq), BF16), field, field, field,
                        pltpu.VMEM((tk, tq), BF16)],
        compiler_params=_params(("parallel", "arbitrary")),
        name="indexer",
    )(pa, iw, ik2)


def _dsa_kernel(qt_ref, kt_ref, q_ref, k_ref, v_ref, m_ref, o_ref, acc_ref, mx_ref, vaug_ref, bias_ref, *,
                t, scale):
    qi = qt_ref[pl.program_id(1)]
    kj = kt_ref[pl.program_id(1)]
    reps = t // LANES

    @pl.when(kj == 0)
    def _():
        acc_ref[...] = jnp.zeros_like(acc_ref)
        mx_ref[...] = jnp.full_like(mx_ref, NEG_BIG)
        vaug_ref[:, :, HEAD_DIM:] = jnp.ones((N_DSA_HEADS, t, LANES), BF16)

    bias_ref[...] = m_ref[...].astype(F32)
    for h in range(N_DSA_HEADS):
        hs = slice(h * HEAD_DIM, (h + 1) * HEAD_DIM)
        vaug_ref[h, :, 0:HEAD_DIM] = v_ref[:, hs]
        s = _dot_nt(q_ref[:, hs], k_ref[:, hs]) * (scale * LOG2_E) + bias_ref[...]
        m_old = mx_ref[h]
        m_new = jnp.maximum(m_old, jnp.max(s, axis=1, keepdims=True))
        alpha = jnp.exp2(m_old - m_new)
        p = jnp.exp2(s - jnp.tile(m_new, (1, reps)))
        acc_ref[h] = jnp.tile(alpha, (1, 2)) * acc_ref[h] + _dot(p.astype(BF16), vaug_ref[h])
        mx_ref[h] = m_new

    @pl.when(kj == qi)
    def _():
        for h in range(N_DSA_HEADS):
            o_ref[:, h * HEAD_DIM:(h + 1) * HEAD_DIM] = (
                acc_ref[h, :, 0:HEAD_DIM] / acc_ref[h, :, HEAD_DIM:]).astype(o_ref.dtype)


def _dsa_attention(pa, bias, B, S, t=512):
    n = S // t
    pairs = [(i, j) for i in range(n) for j in range(i + 1)]
    qt = jnp.asarray([p[0] for p in pairs], jnp.int32)
    kt = jnp.asarray([p[1] for p in pairs], jnp.int32)
    grid_spec = pltpu.PrefetchScalarGridSpec(
        num_scalar_prefetch=2,
        grid=(B, len(pairs)),
        in_specs=[pl.BlockSpec((None, t, DSA_W), lambda b, p, qt, kt: (b, qt[p], 3)),
                  pl.BlockSpec((None, t, DSA_W), lambda b, p, qt, kt: (b, kt[p], 4)),
                  pl.BlockSpec((None, t, DSA_W), lambda b, p, qt, kt: (b, kt[p], 5)),
                  pl.BlockSpec((None, None, t, t), lambda b, p, qt, kt: (b, kt[p], qt[p], 0))],
        out_specs=pl.BlockSpec((None, t, DSA_W), lambda b, p, qt, kt: (b, qt[p], 0)),
        scratch_shapes=[pltpu.VMEM((N_DSA_HEADS, t, 2 * HEAD_DIM), F32),
                        pltpu.VMEM((N_DSA_HEADS, t, LANES), F32),
                        pltpu.VMEM((N_DSA_HEADS, t, 2 * HEAD_DIM), BF16),
                        pltpu.VMEM((t, t), F32)])
    return pl.pallas_call(
        functools.partial(_dsa_kernel, t=t, scale=HEAD_DIM ** -0.5),
        out_shape=jax.ShapeDtypeStruct((B, S, DSA_W), BF16),
        grid_spec=grid_spec,
        compiler_params=_params(("parallel", "arbitrary")),
        name="dsa_attention",
    )(qt, kt, pa, pa, pa, bias)


def _merge_kernel(osb_ref, ods_ref, wsb_ref, wds_ref, gsb_ref, gds_ref, o_ref):
    a = _dot(osb_ref[...], wsb_ref[...])
    b = _dot(ods_ref[...], wds_ref[...])
    o_ref[...] = (jax.nn.sigmoid(gsb_ref[...]) * a + jax.nn.sigmoid(gds_ref[...]) * b).astype(o_ref.dtype)


def _merge(o_sb, o_ds, w_branch, gates, tm=1024, tn=1024):
    T = o_sb.shape[0]
    nj = D_MODEL // tn
    return pl.pallas_call(
        _merge_kernel,
        out_shape=jax.ShapeDtypeStruct((T, D_MODEL), BF16),
        grid=(T // tm, nj),
        in_specs=[pl.BlockSpec((tm, SB_W), lambda i, j: (i, 0)),
                  pl.BlockSpec((tm, DSA_W), lambda i, j: (i, 0)),
                  pl.BlockSpec((SB_W, tn), lambda i, j: (0, j)),
                  pl.BlockSpec((DSA_W, tn), lambda i, j: (SB_W // DSA_W, j)),
                  pl.BlockSpec((tm, tn), lambda i, j: (i, j)),
                  pl.BlockSpec((tm, tn), lambda i, j: (i, nj + j))],
        out_specs=pl.BlockSpec((tm, tn), lambda i, j: (i, j)),
        compiler_params=_params(("parallel", "arbitrary")),
        name="merge",
    )(o_sb, o_ds, w_branch, w_branch, gates, gates)


def _matmul_res_kernel(a_ref, w_ref, r_ref, o_ref):
    o_ref[...] = r_ref[...] + _dot(a_ref[...], w_ref[...])


def _matmul_res(a, w, res, tm=1024, tn=1024, name="matmul_res"):
    T, K = a.shape
    N = w.shape[1]
    return pl.pallas_call(
        _matmul_res_kernel,
        out_shape=jax.ShapeDtypeStruct((T, N), F32),
        grid=(T // tm, N // tn),
        in_specs=[pl.BlockSpec((tm, K), lambda i, j: (i, 0)),
                  pl.BlockSpec((K, tn), lambda i, j: (0, j)),
                  pl.BlockSpec((tm, tn), lambda i, j: (i, j))],
        out_specs=pl.BlockSpec((tm, tn), lambda i, j: (i, j)),
        compiler_params=_params(("parallel", "arbitrary")),
        name=name,
    )(a, w, res)


HALO = PACKED_ROWS


def _ffn_kernel(x_ref, xp_ref, g_ref, wg_ref, wv_ref, cwg_ref, cwv_ref, cbg_ref, cbv_ref, wd_ref, o_ref,
                h_ref, *, tm, rows_per_seq):
    i = pl.program_id(0)
    j = pl.program_id(1)

    def norm(x):
        ms = jnp.mean(x * x, axis=-1, keepdims=True)
        return (x * lax.rsqrt(ms + EPS) * g_ref[...]).astype(BF16)

    @pl.when(j == 0)
    def _():
        x = x_ref[...]
        h_ref[HALO:, :] = norm(x)
        first = (i % (rows_per_seq // tm)) == 0
        hp = norm(xp_ref[...])
        h_ref[0:HALO, :] = jnp.where(first, jnp.zeros_like(hp), hp)
        o_ref[...] = x

    h = h_ref[...]

    def conv(w_ref, cw_ref, cb_ref):
        u = _dot(h, w_ref[...])
        u0 = u[HALO:, :]
        u1 = pltpu.roll(u, 1, 0)[HALO:, :]
        u2 = pltpu.roll(u, 2, 0)[HALO:, :]
        return cw_ref[0:1, :] * u2 + cw_ref[1:2, :] * u1 + cw_ref[2:3, :] * u0 + cb_ref[...]

    gate = conv(wg_ref, cwg_ref, cbg_ref)
    val = conv(wv_ref, cwv_ref, cbv_ref)
    act = (gate * jax.nn.sigmoid(gate) * val).astype(BF16)
    o_ref[...] += _dot(act, wd_ref[...])


def _ffn(x1, g, w_up, conv_w, conv_b, w_down, rows_per_seq, tm=512, tf=512):
    T, D = x1.shape
    nj = D_FF // tf
    cb = conv_b.reshape(1, 2 * D_FF)
    return pl.pallas_call(
        functools.partial(_ffn_kernel, tm=tm, rows_per_seq=rows_per_seq),
        out_shape=jax.ShapeDtypeStruct((T, D), F32),
        grid=(T // tm, nj),
        in_specs=[pl.BlockSpec((tm, D), lambda i, j: (i, 0)),
                  pl.BlockSpec((HALO, D), lambda i, j: (jnp.maximum(i * (tm // HALO) - 1, 0), 0)),
                  pl.BlockSpec((1, D), lambda i, j: (0, 0)),
                  pl.BlockSpec((D, tf), lambda i, j: (0, j)),
                  pl.BlockSpec((D, tf), lambda i, j: (0, nj + j)),
                  pl.BlockSpec((CONV_WIDTH, tf), lambda i, j: (0, j)),
                  pl.BlockSpec((CONV_WIDTH, tf), lambda i, j: (0, nj + j)),
                  pl.BlockSpec((1, tf), lambda i, j: (0, j)),
                  pl.BlockSpec((1, tf), lambda i, j: (0, nj + j)),
                  pl.BlockSpec((tf, D), lambda i, j: (j, 0))],
        out_specs=pl.BlockSpec((tm, D), lambda i, j: (i, 0)),
        scratch_shapes=[pltpu.VMEM((HALO + tm, D), BF16)],
        compiler_params=_params(("parallel", "arbitrary")),
        name="ffn",
    )(x1, x1, g.reshape(1, D), w_up, w_up, conv_w, conv_w, cb, cb, w_down)


def _rope_expansion():
    e = np.zeros((ROT_SRC, 6 * LANES), np.float32)
    c = np.zeros((1, 6 * LANES), np.float32)
    src = 0
    for t0, (rot, width) in ((0, (ROPE_DIM, HEAD_DIM)), (3, (IDX_ROPE_DIM, IDX_DIM))):
        half = rot // 2
        cos0, sin0 = src, src + half
        src += rot
        for g in range(LANES // width):
            base = g * width
            for i in range(half):
                e[cos0 + i, t0 * LANES + base + i] = 1.0
                e[cos0 + i, t0 * LANES + base + half + i] = 1.0
                e[sin0 + i, (t0 + 1) * LANES + base + half + i] = 1.0
                e[sin0 + i, (t0 + 2) * LANES + base + i] = -1.0
            c[0, t0 * LANES + base + rot:t0 * LANES + base + width] = 1.0
    return jnp.asarray(e, BF16), jnp.asarray(c, F32)


def _rope_compact(positions):
    pos = positions.reshape(-1).astype(F32)[:, None]
    is_cos = np.zeros((ROT_SRC,), bool)
    lane_freq = []
    src = 0
    for rot in (ROPE_DIM, IDX_ROPE_DIM):
        f = ROPE_THETA ** (-jnp.arange(0, rot, 2, dtype=F32) / rot)
        lane_freq += [f, f]
        is_cos[src:src + rot // 2] = True
        src += rot
    lane_freq = jnp.concatenate(lane_freq + [jnp.zeros((ROT_SRC - src,), F32)])
    ang = pos * lane_freq[None, :]
    cs = jnp.where(jnp.asarray(is_cos)[None, :], jnp.cos(ang), jnp.sin(ang))

    def top16(v):
        return lax.bitcast_convert_type(lax.bitcast_convert_type(v, jnp.int32) & jnp.int32(-65536), F32)

    hi = top16(cs)
    r = cs - hi
    mid = top16(r)
    lo = r - mid
    return jnp.concatenate([hi, mid, lo], axis=1).astype(BF16)


def kernel(x, positions, attn_norm_g, w_in, q_norm_g, k_norm_g, w_branch, w_out, ffn_norm_g, w_up,
           conv_w, conv_b, w_down):
    B, S, D = x.shape
    assert D == D_MODEL and S % 1024 == 0
    T = B * S
    topk = min(MAX_TOPK, S // 4)
    xf = x.reshape(T, D)

    o_qix = 3 * SB_W + 3 * DSA_W
    o_kix = o_qix + IDX_QW
    o_wix = o_kix + IDX_DIM
    o_g = o_wix + N_IDX_HEADS
    w_in_bf = w_in[:, :(o_g + LANES - 1) // LANES * LANES].astype(BF16)
    w_kix = w_in_bf[:, o_kix:o_wix]
    w_idx = jnp.concatenate([w_kix, w_kix, w_in_bf[:, o_wix:o_g],
                             jnp.zeros((D, LANES - N_IDX_HEADS), BF16)], axis=1)
    w_gate = w_in[:, o_g:].astype(BF16)

    cs = _rope_compact(positions)
    rope_e, rope_c = _rope_expansion()

    pa, h1 = _proj_attn(xf, attn_norm_g, w_in_bf, o_kix, cs, rope_e, rope_c, q_norm_g, k_norm_g)
    gates = _matmul(h1, w_gate, F32, name="proj_gates")
    ik2, iw = _proj_idx(h1, w_idx, cs, rope_e, rope_c)

    pa3 = pa.reshape(B, S, pa.shape[1])
    o_sb = _sb_attention(pa3, B, S)
    bias = _indexer(pa3, iw.reshape(B, S, LANES), ik2.reshape(B, S, LANES), B, S, topk)
    o_ds = _dsa_attention(pa3, bias, B, S)

    merged = _merge(o_sb.reshape(T, SB_W), o_ds.reshape(T, DSA_W), w_branch.astype(BF16), gates)
    x1 = _matmul_res(merged, w_out.astype(BF16), xf, name="out_proj")

    out = _ffn(x1, ffn_norm_g, w_up.astype(BF16), conv_w, conv_b, w_down.astype(BF16), S)
    return out.reshape(B, S, D)
```
